```python
import math
import jax, jax.numpy as jnp
from jax import lax
import numpy as np

D_MODEL = 1024
BATCH = 4
SEQ = 4096
DEPTH = 1

CHUNK = 64

N_HEADS = D_MODEL // 128
HEAD_DIM = 64
ATT_W = N_HEADS * HEAD_DIM
Q_BLOCK = 128

CONV_C = D_MODEL // 2
CONV_K = 31

PEER_HEADS = 8
PEER_DK = 128
PEER_DHALF = PEER_DK // 2
N_KEYS = 128
N_EXPERTS = N_KEYS * N_KEYS
PEER_TOPK = 16
PEER_BLOCK = 128

ALPHA = (2.0 * DEPTH) ** 0.25
BETA = (8.0 * DEPTH) ** -0.25
LN_EPS = 1e-5

SPLIT_SIZES = (ATT_W, ATT_W, ATT_W, N_HEADS, CONV_C, CONV_C, D_MODEL, D_MODEL)
IN_WIDTH = int(sum(SPLIT_SIZES))

kernel_name = "hybrid_fox_conformer_peer_block"


def layer_norm(x, gain=None, bias=None):
    xf = x.astype(jnp.float32)
    mu = jnp.mean(xf, axis=-1, keepdims=True)
    var = jnp.mean(jnp.square(xf - mu), axis=-1, keepdims=True)
    y = (xf - mu) * lax.rsqrt(var + LN_EPS)
    if gain is not None:
        y = y * gain.astype(jnp.float32) + bias.astype(jnp.float32)
    return y.astype(x.dtype)


def modulate(x, shift, scale):
    return layer_norm(x) * (1.0 + scale) + shift


def fox_attention(q, k, v, logf):
    B, S, H, Dh = q.shape
    nblk = S // Q_BLOCK
    scale = 1.0 / math.sqrt(Dh)
    F = jnp.cumsum(logf.astype(jnp.float32), axis=1).transpose(0, 2, 1)
    qh = q.transpose(0, 2, 1, 3)
    kh = k.transpose(0, 2, 1, 3)
    vh = v.transpose(0, 2, 1, 3)
    qb = qh.reshape(B, H, nblk, Q_BLOCK, Dh).transpose(2, 0, 1, 3, 4)
    Fb = F.reshape(B, H, nblk, Q_BLOCK).transpose(2, 0, 1, 3)
    key_pos = jnp.arange(S)

    def one_block(args):
        qi, Fi, bi = args
        s = jnp.einsum('bhqd,bhkd->bhqk', qi, kh).astype(jnp.float32) * scale
        s = s + Fi[..., :, None] - F[..., None, :]
        q_pos = bi * Q_BLOCK + jnp.arange(Q_BLOCK)
        s = jnp.where(key_pos[None, :] <= q_pos[:, None], s, -jnp.inf)
        p = jax.nn.softmax(s, axis=-1).astype(vh.dtype)
        return jnp.einsum('bhqk,bhkd->bhqd', p, vh)

    o = lax.map(one_block, (qb, Fb, jnp.arange(nblk)))
    return o.transpose(1, 0, 3, 2, 4).reshape(B, S, H * Dh)


def causal_depthwise_conv(x, w, b):
    C = x.shape[-1]
    xp = jnp.pad(x, ((0, 0), (CONV_K - 1, 0), (0, 0)))
    y = lax.conv_general_dilated(xp, w[:, None, :].astype(x.dtype), window_strides=(1,),
                                 padding='VALID', dimension_numbers=('NWC', 'WIO', 'NWC'),
                                 feature_group_count=C)
    return y + b


def peer(h, w_q, sub_keys, u_tab, v_tab):
    B, S, D = h.shape
    T = B * S
    hf = h.reshape(T, D)
    q = (hf @ w_q).reshape(T, PEER_HEADS, 2, PEER_DHALF).astype(jnp.float32)
    scores = jnp.einsum('thpd,pnd->thpn', q, sub_keys.astype(jnp.float32))
    sv, si = lax.top_k(scores, PEER_TOPK)
    cand = (sv[:, :, 0, :, None] + sv[:, :, 1, None, :]).reshape(T, PEER_HEADS, PEER_TOPK * PEER_TOPK)
    cv, ci = lax.top_k(cand, PEER_TOPK)
    ia = ci // PEER_TOPK
    ib = ci % PEER_TOPK
    e_idx = (jnp.take_along_axis(si[:, :, 0], ia, axis=-1) * N_KEYS
             + jnp.take_along_axis(si[:, :, 1], ib, axis=-1))
    gates = jax.nn.softmax(cv, axis=-1)

    nb = T // PEER_BLOCK
    xb = hf.reshape(nb, PEER_BLOCK, D)
    eb = e_idx.reshape(nb, PEER_BLOCK, PEER_HEADS, PEER_TOPK)
    gb = gates.reshape(nb, PEER_BLOCK, PEER_HEADS, PEER_TOPK)

    def one_block(args):
        xi, ei, gi = args
        ue = jnp.take(u_tab, ei, axis=0)
        a = jnp.einsum('thkd,td->thk', ue, xi)
        wgt = (jax.nn.gelu(a.astype(jnp.float32)) * gi).astype(xi.dtype)
        ve = jnp.take(v_tab, ei, axis=0)
        return jnp.einsum('thk,thkd->td', wgt, ve)

    out = lax.map(one_block, (xb, eb, gb))
    return out.reshape(B, S, D)


def setup_inputs(seed: int = 0) -> dict:
    key = jax.random.key(seed)
    ks = jax.random.split(key, 24)
    f32 = jnp.float32
    nrm = lambda k, shp: jax.random.normal(k, shp, f32)
    L, D = DEPTH, D_MODEL
    return {
        "x": nrm(ks[0], (BATCH, SEQ, D)),
        "c": nrm(ks[1], (BATCH, D)),
        "w_mod": nrm(ks[2], (L, D, 6 * D)) * (0.5 * D ** -0.5),
        "b_mod": nrm(ks[3], (L, 6 * D)) * 0.02,
        "w_in": nrm(ks[4], (L, D, IN_WIDTH)) * D ** -0.5,
        "b_f": jax.random.uniform(ks[5], (L, N_HEADS), f32, 1.0, 5.0),
        "conv_w": nrm(ks[6], (L, CONV_K, CONV_C)) * CONV_K ** -0.5,
        "conv_b": nrm(ks[7], (L, CONV_C)) * 0.02,
        "conv_ln_g": 1.0 + 0.02 * nrm(ks[8], (L, CONV_C)),
        "conv_ln_b": 0.02 * nrm(ks[9], (L, CONV_C)),
        "w_attn_out": nrm(ks[10], (L, ATT_W, D)) * (ATT_W ** -0.5 * BETA),
        "w_conv_out": nrm(ks[11], (L, CONV_C, D)) * (CONV_C ** -0.5 * BETA),
        "w_out": nrm(ks[12], (L, D, D)) * (D ** -0.5 * BETA),
        "ln1_g": 1.0 + 0.02 * nrm(ks[13], (L, D)),
        "ln1_b": 0.02 * nrm(ks[14], (L, D)),
        "peer_wq": nrm(ks[15], (L, D, PEER_HEADS * PEER_DK)) * D ** -0.5,
        "peer_sub_keys": nrm(ks[16], (L, 2, N_KEYS, PEER_DHALF)) * PEER_DHALF ** -0.5,
        "peer_u": nrm(ks[17], (L, N_EXPERTS, D)) * D ** -0.5,
        "peer_v": nrm(ks[18], (L, N_EXPERTS, D)) * BETA,
        "ln2_g": 1.0 + 0.02 * nrm(ks[19], (L, D)),
        "ln2_b": 0.02 * nrm(ks[20], (L, D)),
    }


def reference(x, c, w_mod, b_mod, w_in, b_f, conv_w, conv_b, conv_ln_g, conv_ln_b,
              w_attn_out, w_conv_out, w_out, ln1_g, ln1_b, peer_wq, peer_sub_keys,
              peer_u, peer_v, ln2_g, ln2_b):
    B, S, D = x.shape
    offsets = [int(o) for o in np.cumsum(SPLIT_SIZES)[:-1]]
    for l in range(DEPTH):
        mod = (jax.nn.silu(c) @ w_mod[l] + b_mod[l])[:, None, :]
        sh1, sc1, g1, sh2, sc2, g2 = jnp.split(mod, 6, axis=-1)

        u = modulate(x, sh1, sc1)
        z = u @ w_in[l]
        q, k, v, f_logit, ca, cb, ga, gb = jnp.split(z, offsets, axis=-1)
        logf = jax.nn.log_sigmoid((f_logit + b_f[l]).astype(jnp.float32))
        attn = fox_attention(q.reshape(B, S, N_HEADS, HEAD_DIM),
                             k.reshape(B, S, N_HEADS, HEAD_DIM),
                             v.reshape(B, S, N_HEADS, HEAD_DIM), logf)
        y_a = attn @ w_attn_out[l]

        hconv = ca * jax.nn.sigmoid(cb)
        hconv = causal_depthwise_conv(hconv, conv_w[l], conv_b[l])
        hconv = jax.nn.silu(layer_norm(hconv, conv_ln_g[l], conv_ln_b[l]))
        y_b = hconv @ w_conv_out[l]

        merged = jax.nn.sigmoid(ga) * y_a + jax.nn.sigmoid(gb) * y_b
        out = merged @ w_out[l]
        x = layer_norm(ALPHA * x + (1.0 + g1) * out, ln1_g[l], ln1_b[l])

        u2 = modulate(x, sh2, sc2)
        y_p = peer(u2, peer_wq[l], peer_sub_keys[l], peer_u[l], peer_v[l])
        x = layer_norm(ALPHA * x + (1.0 + g2) * y_p, ln2_g[l], ln2_b[l])
    return x
```

```python
import functools
import math

import jax
import jax.numpy as jnp
import numpy as np
from jax import lax
from jax.experimental import pallas as pl
from jax.experimental.pallas import tpu as pltpu

F32 = jnp.float32
BF16 = jnp.bfloat16
I32 = jnp.int32

LN_EPS = 1e-5
PEER_TOPK = 16
LANES = 128
VMEM_LIMIT_BYTES = 56 * 1024 * 1024
NEG_INF = float("-inf")


def _params(*semantics):
    return pltpu.CompilerParams(dimension_semantics=semantics, vmem_limit_bytes=VMEM_LIMIT_BYTES)


def _normalize(x):
    mu = jnp.mean(x, axis=-1, keepdims=True)
    xc = x - mu
    var = jnp.mean(xc * xc, axis=-1, keepdims=True)
    return xc * lax.rsqrt(var + LN_EPS)


def _resident(shape):
    nd = len(shape)
    return pl.BlockSpec(shape, lambda *_: (0,) * nd)


def _mod_kernel(c_ref, w_ref, b_ref, o_ref):
    c = c_ref[...]
    cs = c * jax.nn.sigmoid(c)
    o_ref[...] = jnp.dot(cs, w_ref[...], preferred_element_type=F32,
                         precision=lax.Precision.HIGHEST) + b_ref[...]


def _mod(c, w, b):
    bsz, d = c.shape
    n = w.shape[1]
    rows = -(-bsz // 8) * 8
    tn = math.gcd(n, 1536)
    cp = jnp.pad(c, ((0, rows - bsz), (0, 0)))
    out = pl.pallas_call(
        _mod_kernel,
        grid=(n // tn,),
        in_specs=[pl.BlockSpec((rows, d), lambda j: (0, 0)),
                  pl.BlockSpec((d, tn), lambda j: (0, j)),
                  pl.BlockSpec((1, tn), lambda j: (0, j))],
        out_specs=pl.BlockSpec((rows, tn), lambda j: (0, j)),
        out_shape=jax.ShapeDtypeStruct((rows, n), F32),
        compiler_params=_params("arbitrary"),
        name="mod",
    )(cp, w, b.reshape(1, n))
    return out[:bsz]


def _inproj_kernel(x_ref, sh_ref, sc_ref, wqkv_ref, wf_ref, wc_ref, wg_ref,
                   qkv_ref, f_ref, h_ref, g_ref, *, att_w, conv_c, qscale):
    y = _normalize(x_ref[...])
    u = (y * (1.0 + sc_ref[...]) + sh_ref[...]).astype(BF16)
    qkv = jnp.dot(u, wqkv_ref[...], preferred_element_type=F32)
    qkv_ref[:, :att_w] = (qkv[:, :att_w] * qscale).astype(BF16)
    qkv_ref[:, att_w:] = qkv[:, att_w:].astype(BF16)
    f_ref[...] = jnp.dot(u, wf_ref[...], preferred_element_type=F32)
    cab = jnp.dot(u, wc_ref[...], preferred_element_type=F32)
    h_ref[...] = cab[:, :conv_c] * jax.nn.sigmoid(cab[:, conv_c:])
    g_ref[...] = jax.nn.sigmoid(jnp.dot(u, wg_ref[...], preferred_element_type=F32))


def _inproj(x2, sh, sc, wqkv, wf, wc, wg, *, seq, att_w, conv_c, head_dim, tm):
    t, d = x2.shape
    tpb = seq // tm
    row = lambda i: (i, 0)
    per_batch = pl.BlockSpec((None, 1, d), lambda i: (i // tpb, 0, 0))
    kern = functools.partial(_inproj_kernel, att_w=att_w, conv_c=conv_c,
                             qscale=1.0 / math.sqrt(head_dim))
    return pl.pallas_call(
        kern,
        grid=(t // tm,),
        in_specs=[pl.BlockSpec((tm, d), row), per_batch, per_batch,
                  _resident(wqkv.shape), _resident(wf.shape), _resident(wc.shape),
                  _resident(wg.shape)],
        out_specs=[pl.BlockSpec((tm, 3 * att_w), row), pl.BlockSpec((tm, LANES), row),
                   pl.BlockSpec((tm, conv_c), row), pl.BlockSpec((tm, 2 * d), row)],
        out_shape=[jax.ShapeDtypeStruct((t, 3 * att_w), BF16),
                   jax.ShapeDtypeStruct((t, LANES), F32),
                   jax.ShapeDtypeStruct((t, conv_c), F32),
                   jax.ShapeDtypeStruct((t, 2 * d), F32)],
        compiler_params=_params("arbitrary"),
        name="inproj",
    )(x2, sh, sc, wqkv, wf, wc, wg)


def _fcum_kernel(f_ref, bf_ref, o_ref, carry_ref, *, tc):
    @pl.when(pl.program_id(1) == 0)
    def _():
        carry_ref[...] = jnp.zeros(carry_ref.shape, F32)

    z = f_ref[...] + bf_ref[...]
    logf = jnp.minimum(z, 0.0) - jnp.log1p(jnp.exp(-jnp.abs(z)))
    r = lax.broadcasted_iota(I32, (tc, tc), 0)
    c = lax.broadcasted_iota(I32, (tc, tc), 1)
    tri = jnp.where(c <= r, 1.0, 0.0).astype(F32)
    cs = jnp.dot(tri, logf, preferred_element_type=F32,
                 precision=lax.Precision.HIGHEST) + carry_ref[...]
    o_ref[...] = cs
    carry_ref[...] = cs[tc - 1:tc, :]


def _fcum(f3, bf_row, *, tc):
    bsz, seq, w = f3.shape
    return pl.pallas_call(
        functools.partial(_fcum_kernel, tc=tc),
        grid=(bsz, seq // tc),
        in_specs=[pl.BlockSpec((None, tc, w), lambda b, j: (b, j, 0)),
                  pl.BlockSpec((1, w), lambda b, j: (0, 0))],
        out_specs=pl.BlockSpec((None, tc, w), lambda b, j: (b, j, 0)),
        out_shape=jax.ShapeDtypeStruct((bsz, seq, w), F32),
        scratch_shapes=[pltpu.VMEM((1, w), F32)],
        compiler_params=_params("arbitrary", "arbitrary"),
        name="fcum",
    )(f3, bf_row)


ATTN_MASKED = 1
ATTN_LAST = 2


def _attn_tables(nq, tq, tk):
    qi_l, ki_l, fl_l = [], [], []
    for qi in range(nq):
        last = (qi * tq + tq - 1) // tk
        for ki in range(last + 1):
            masked = ki * tk + tk - 1 > qi * tq
            qi_l.append(qi)
            ki_l.append(ki)
            fl_l.append((ATTN_MASKED if masked else 0) | (ATTN_LAST if ki == last else 0))
    as_i32 = lambda v: jnp.asarray(np.asarray(v, np.int32))
    return as_i32(qi_l), as_i32(ki_l), as_i32(fl_l)


def _attn_kernel(qi_tab, ki_tab, fl_tab, q_ref, k_ref, v_ref, fq_ref, fk_ref, o_ref,
                 m_sc, l_sc, acc_sc, *, tq, tk, dh, heads):
    p = pl.program_id(2)
    qi = qi_tab[p]
    ki = ki_tab[p]
    fl = fl_tab[p]

    @pl.when(ki == 0)
    def _():
        m_sc[...] = jnp.full(m_sc.shape, NEG_INF, F32)
        l_sc[...] = jnp.zeros(l_sc.shape, F32)
        acc_sc[...] = jnp.zeros(acc_sc.shape, F32)

    def step(masked):
        if masked:
            row = qi * tq + lax.broadcasted_iota(I32, (tq, tk), 0)
            col = ki * tk + lax.broadcasted_iota(I32, (tq, tk), 1)
            keep = col <= row
        for hh in range(heads):
            sl = slice(hh * dh, (hh + 1) * dh)
            s = lax.dot_general(q_ref[:, sl], k_ref[:, sl], (((1,), (1,)), ((), ())),
                                preferred_element_type=F32)
            s = s + fq_ref[hh] - fk_ref[hh]
            if masked:
                s = jnp.where(keep, s, NEG_INF)
            m_prev = m_sc[hh]
            m_new = jnp.maximum(m_prev, jnp.max(s, axis=-1, keepdims=True))
            alpha = jnp.exp(m_prev - m_new)
            pr = jnp.exp(s - m_new)
            l_sc[hh] = alpha * l_sc[hh] + jnp.sum(pr, axis=-1, keepdims=True)
            acc_sc[hh] = alpha * acc_sc[hh] + jnp.dot(pr.astype(BF16), v_ref[:, sl],
                                                      preferred_element_type=F32)
            m_sc[hh] = m_new

    pl.when((fl & ATTN_MASKED) != 0)(lambda: step(True))
    pl.when((fl & ATTN_MASKED) == 0)(lambda: step(False))

    @pl.when((fl & ATTN_LAST) != 0)
    def _():
        outs = [acc_sc[hh] / l_sc[hh] for hh in range(heads)]
        o_ref[...] = jnp.concatenate(outs, axis=-1).astype(o_ref.dtype)


def _attention(qkv, fq, fk, *, bsz, seq, att_w, head_dim, tq, tk):
    heads = LANES // head_dim
    ncol = att_w // LANES
    nq, nk = seq // tq, seq // tk
    qi_tab, ki_tab, fl_tab = _attn_tables(nq, tq, tk)
    kern = functools.partial(_attn_kernel, tq=tq, tk=tk, dh=head_dim, heads=heads)
    grid_spec = pltpu.PrefetchScalarGridSpec(
        num_scalar_prefetch=3,
        grid=(bsz, ncol, int(qi_tab.shape[0])),
        in_specs=[
            pl.BlockSpec((tq, LANES), lambda b, c, p, qt, kt, ft: (b * nq + qt[p], c)),
            pl.BlockSpec((tk, LANES), lambda b, c, p, qt, kt, ft: (b * nk + kt[p], ncol + c)),
            pl.BlockSpec((tk, LANES), lambda b, c, p, qt, kt, ft: (b * nk + kt[p], 2 * ncol + c)),
            pl.BlockSpec((None, heads, tq, 1), lambda b, c, p, qt, kt, ft: (b, c, qt[p], 0)),
            pl.BlockSpec((None, heads, 1, tk), lambda b, c, p, qt, kt, ft: (b, c, 0, kt[p])),
        ],
        out_specs=pl.BlockSpec((tq, LANES), lambda b, c, p, qt, kt, ft: (b * nq + qt[p], c)),
        scratch_shapes=[pltpu.VMEM((heads, tq, 1), F32), pltpu.VMEM((heads, tq, 1), F32),
                        pltpu.VMEM((heads, tq, head_dim), F32)],
    )
    return pl.pallas_call(
        kern,
        grid_spec=grid_spec,
        out_shape=jax.ShapeDtypeStruct((bsz * seq, att_w), BF16),
        compiler_params=_params("arbitrary", "arbitrary", "arbitrary"),
        name="attn",
    )(qi_tab, ki_tab, fl_tab, qkv, qkv, qkv, fq, fk)


CONV_HALO = 32
CONV_CHUNK = 64


def _conv_kernel(h_ref, halo_ref, w_ref, b_ref, g_ref, beta_ref, o_ref, pad_sc, *, ts, taps, tpb):
    first = pl.program_id(0) % tpb == 0
    pad_sc[0:CONV_HALO, :] = jnp.where(first, 0.0, halo_ref[...])
    pad_sc[CONV_HALO:, :] = h_ref[...]
    lead = CONV_HALO - (taps - 1)
    for c0 in range(0, ts, CONV_CHUNK):
        acc = jnp.broadcast_to(b_ref[...], (CONV_CHUNK, b_ref.shape[1]))
        for k in range(taps):
            acc = acc + w_ref[k:k + 1, :] * pad_sc[c0 + lead + k:c0 + lead + k + CONV_CHUNK, :]
        y = _normalize(acc) * g_ref[...] + beta_ref[...]
        o_ref[c0:c0 + CONV_CHUNK, :] = (y * jax.nn.sigmoid(y)).astype(o_ref.dtype)


def _conv(h, w, b, g, beta, *, seq, ts):
    t, c = h.shape
    taps = w.shape[0]
    assert taps - 1 <= CONV_HALO and ts % CONV_CHUNK == 0 and ts % CONV_HALO == 0
    tpb = seq // ts
    hb = ts // CONV_HALO
    vec = lambda v: v.reshape(1, c)
    return pl.pallas_call(
        functools.partial(_conv_kernel, ts=ts, taps=taps, tpb=tpb),
        grid=(t // ts,),
        in_specs=[pl.BlockSpec((ts, c), lambda i: (i, 0)),
                  pl.BlockSpec((CONV_HALO, c), lambda i: (jnp.maximum(i * hb - 1, 0), 0)),
                  _resident((taps, c)), _resident((1, c)), _resident((1, c)), _resident((1, c))],
        out_specs=pl.BlockSpec((ts, c), lambda i: (i, 0)),
        out_shape=jax.ShapeDtypeStruct((t, c), BF16),
        scratch_shapes=[pltpu.VMEM((ts + CONV_HALO, c), F32)],
        compiler_params=_params("arbitrary"),
        name="conv",
    )(h, h, w, vec(b), vec(g), vec(beta))


def _merge_kernel(attn_ref, hc_ref, g_ref, x_ref, g1_ref, sh2_ref, sc2_ref, wa_ref, wc_ref,
                  wo_ref, lg_ref, lb_ref, wq_ref, x1_ref, u2_ref, qp_ref, *, alpha):
    d = x_ref.shape[1]
    ya = jnp.dot(attn_ref[...], wa_ref[...], preferred_element_type=F32)
    yb = jnp.dot(hc_ref[...], wc_ref[...], preferred_element_type=F32)
    merged = (g_ref[:, :d] * ya + g_ref[:, d:] * yb).astype(BF16)
    out = jnp.dot(merged, wo_ref[...], preferred_element_type=F32)
    x1 = _normalize(alpha * x_ref[...] + (1.0 + g1_ref[...]) * out) * lg_ref[...] + lb_ref[...]
    x1_ref[...] = x1
    u2 = (_normalize(x1) * (1.0 + sc2_ref[...]) + sh2_ref[...]).astype(BF16)
    u2_ref[...] = u2
    qp_ref[...] = jnp.dot(u2, wq_ref[...], preferred_element_type=F32)


def _merge(attn, hc, gates, x2, g1, sh2, sc2, wa, wc, wo, lg, lb, wq, *, seq, alpha, tm):
    t, d = x2.shape
    tpb = seq // tm
    row = lambda i: (i, 0)
    per_batch = pl.BlockSpec((None, 1, d), lambda i: (i // tpb, 0, 0))
    nq = wq.shape[1]
    return pl.pallas_call(
        functools.partial(_merge_kernel, alpha=alpha),
        grid=(t // tm,),
        in_specs=[pl.BlockSpec((tm, attn.shape[1]), row), pl.BlockSpec((tm, hc.shape[1]), row),
                  pl.BlockSpec((tm, 2 * d), row), pl.BlockSpec((tm, d), row),
                  per_batch, per_batch, per_batch,
                  _resident(wa.shape), _resident(wc.shape), _resident(wo.shape),
                  _resident((1, d)), _resident((1, d)), _resident(wq.shape)],
        out_specs=[pl.BlockSpec((tm, d), row), pl.BlockSpec((tm, d), row),
                   pl.BlockSpec((tm, nq), row)],
        out_shape=[jax.ShapeDtypeStruct((t, d), F32), jax.ShapeDtypeStruct((t, d), BF16),
                   jax.ShapeDtypeStruct((t, nq), F32)],
        compiler_params=_params("arbitrary"),
        name="merge",
    )(attn, hc, gates, x2, g1, sh2, sc2, wa, wc, wo, lg.reshape(1, d), lb.reshape(1, d), wq)


def _staircase(k):
    return [(a, b) for a in range(k) for b in range(k) if (a + 1) * (b + 1) <= k]


def _extract_topk(s, ids, k):
    big = jnp.iinfo(jnp.int32).max
    vals, idxs = [], []
    for _ in range(k):
        m = jnp.max(s, axis=0, keepdims=True)
        i = jnp.min(jnp.where(s == m, ids, big), axis=0, keepdims=True)
        s = jnp.where(ids == i, NEG_INF, s)
        vals.append(m)
        idxs.append(i)
    return jnp.concatenate(vals, axis=0), jnp.concatenate(idxs, axis=0)


def _retrieve_kernel(qp_ref, keys_ref, flat_ref, ii_ref, jj_ref, gg_ref, *, dhalf, nkeys, topk):
    tl = qp_ref.shape[0]
    qb = qp_ref[...].astype(BF16)
    key_ids = lax.broadcasted_iota(I32, (nkeys, tl), 0)
    sv, si = [], []
    for half in range(2):
        s = lax.dot_general(keys_ref[half], qb[:, half * dhalf:(half + 1) * dhalf],
                            (((1,), (1,)), ((), ())), preferred_element_type=F32)
        v, i = _extract_topk(s, key_ids, topk)
        sv.append(v)
        si.append(i)

    pairs = _staircase(topk)
    npad = flat_ref.shape[0]
    cand = [sv[0][a:a + 1, :] + sv[1][b:b + 1, :] for a, b in pairs]
    cand += [jnp.full((1, tl), NEG_INF, F32)] * (npad - len(pairs))
    cand = jnp.concatenate(cand, axis=0)
    cv, ci = _extract_topk(cand, flat_ref[...], topk)

    rank = lax.broadcasted_iota(I32, (topk, tl), 0)
    ia = lax.shift_right_logical(ci, int(math.log2(topk)))
    ib = ci & (topk - 1)
    ii_rows, jj_rows = [], []
    for r in range(topk):
        ii_rows.append(jnp.sum(jnp.where(rank == ia[r:r + 1, :], si[0], 0), axis=0, keepdims=True))
        jj_rows.append(jnp.sum(jnp.where(rank == ib[r:r + 1, :], si[1], 0), axis=0, keepdims=True))
    ii_ref[...] = jnp.concatenate(ii_rows, axis=0)
    jj_ref[...] = jnp.concatenate(jj_rows, axis=0)
    e = jnp.exp(cv - cv[0:1, :])
    gg_ref[...] = e / jnp.sum(e, axis=0, keepdims=True)


def _retrieve(qp, keys, *, tl):
    t, nq = qp.shape
    _, nkeys, dhalf = keys.shape
    heads = nq // (2 * dhalf)
    assert 2 * dhalf == LANES and PEER_TOPK & (PEER_TOPK - 1) == 0
    pairs = _staircase(PEER_TOPK)
    npad = -(-len(pairs) // 8) * 8
    flat = np.full((npad, tl), np.iinfo(np.int32).max - 1, np.int32)
    flat[:len(pairs), :] = np.asarray([a * PEER_TOPK + b for a, b in pairs], np.int32)[:, None]
    kern = functools.partial(_retrieve_kernel, dhalf=dhalf, nkeys=nkeys, topk=PEER_TOPK)
    out = jax.ShapeDtypeStruct((heads * PEER_TOPK, t), I32)
    blk = pl.BlockSpec((PEER_TOPK, tl), lambda i, h: (h, i))
    return pl.pallas_call(
        kern,
        grid=(t // tl, heads),
        in_specs=[pl.BlockSpec((tl, LANES), lambda i, h: (i, h)), _resident(keys.shape),
                  _resident(flat.shape)],
        out_specs=[blk, blk, blk],
        out_shape=[out, out, jax.ShapeDtypeStruct((heads * PEER_TOPK, t), F32)],
        compiler_params=_params("arbitrary", "arbitrary"),
        name="retrieve",
    )(qp, keys, jnp.asarray(flat))


def _gelu_tanh(a):
    return 0.5 * a * (1.0 + jnp.tanh(math.sqrt(2.0 / math.pi) * (a + 0.044715 * a * a * a)))


def _peer_act_kernel(x_ref, u_ref, ii_ref, jj_ref, a_ref, *, groups):
    c = pl.program_id(1)

    @pl.when(c == 0)
    def _():
        a_ref[...] = jnp.zeros(a_ref.shape, F32)

    a_all = lax.dot_general(x_ref[...], u_ref[...], (((1,), (1,)), ((), ())),
                            preferred_element_type=F32)
    ii = ii_ref[...]
    jj = jj_ref[...]
    acc = a_ref[...]
    for g in range(groups):
        picked = jnp.take_along_axis(a_all[:, g * LANES:(g + 1) * LANES], jj, axis=1)
        acc = jnp.where(ii == c * groups + g, picked, acc)
    a_ref[...] = acc


def _peer_act(u2, utab, ii, jj, *, tb, groups):
    t, d = u2.shape
    ne = utab.shape[0]
    ce = groups * LANES
    tok = lambda i, c: (i, 0)
    return pl.pallas_call(
        functools.partial(_peer_act_kernel, groups=groups),
        grid=(t // tb, ne // ce),
        in_specs=[pl.BlockSpec((tb, d), tok), pl.BlockSpec((ce, d), lambda i, c: (c, 0)),
                  pl.BlockSpec((tb, LANES), tok), pl.BlockSpec((tb, LANES), tok)],
        out_specs=pl.BlockSpec((tb, LANES), tok),
        out_shape=jax.ShapeDtypeStruct((t, LANES), F32),
        compiler_params=_params("arbitrary", "arbitrary"),
        name="peer_act",
    )(u2, utab, ii, jj)


def _w_pitch(tb):
    p = -(-tb // 8)
    return 8 * (p if p % 2 else p + 1)


def _peer_out_kernel(a_ref, gg_ref, ii_ref, jj_ref, v_ref, x1_ref, g2_ref, lg_ref, lb_ref,
                     o_ref, w_sc, acc_sc, *, tb, groups, nchunks, alpha):
    c = pl.program_id(1)
    pitch = _w_pitch(tb)

    @pl.when(c == 0)
    def _():
        acc_sc[...] = jnp.zeros(acc_sc.shape, F32)
        sub = lax.broadcasted_iota(I32, (LANES, LANES), 0)

        def scatter(tok, carry):
            wrow = _gelu_tanh(a_ref[pl.ds(tok, 1), :]) * gg_ref[pl.ds(tok, 1), :]
            lhs = jnp.where(sub == ii_ref[pl.ds(tok, 1), :], wrow, 0.0).astype(BF16)
            rhs = jnp.where(sub == jj_ref[pl.ds(tok, 1), :], 1.0, 0.0).astype(BF16)
            w_sc[pl.ds(tok, LANES, stride=pitch), :] = lax.dot_general(
                lhs, rhs, (((1,), (1,)), ((), ())), preferred_element_type=F32)
            return carry

        lax.fori_loop(0, tb, scatter, 0)

    parts = [w_sc[pl.ds(pl.multiple_of((c * groups + g) * pitch, 8), tb), :].astype(BF16)
             for g in range(groups)]
    acc_sc[...] += jnp.dot(jnp.concatenate(parts, axis=1), v_ref[...],
                           preferred_element_type=F32)

    @pl.when(c == nchunks - 1)
    def _():
        r = alpha * x1_ref[...] + (1.0 + g2_ref[...]) * acc_sc[...]
        o_ref[...] = _normalize(r) * lg_ref[...] + lb_ref[...]


def _peer_out(a, gg, ii, jj, vtab, x1, g2, lg, lb, *, seq, alpha, tb, groups):
    t, d = x1.shape
    ne = vtab.shape[0]
    assert ne == LANES * LANES
    ce = groups * LANES
    nchunks = ne // ce
    tpb = seq // tb
    tok = lambda i, c: (i, 0)
    kern = functools.partial(_peer_out_kernel, tb=tb, groups=groups, nchunks=nchunks, alpha=alpha)
    return pl.pallas_call(
        kern,
        grid=(t // tb, nchunks),
        in_specs=[pl.BlockSpec((tb, LANES), tok), pl.BlockSpec((tb, LANES), tok),
                  pl.BlockSpec((tb, LANES), tok), pl.BlockSpec((tb, LANES), tok),
                  pl.BlockSpec((ce, d), lambda i, c: (c, 0)),
                  pl.BlockSpec((tb, d), tok),
                  pl.BlockSpec((None, 1, d), lambda i, c: (i // tpb, 0, 0)),
                  _resident((1, d)), _resident((1, d))],
        out_specs=pl.BlockSpec((tb, d), tok),
        out_shape=jax.ShapeDtypeStruct((t, d), F32),
        scratch_shapes=[pltpu.VMEM((LANES * _w_pitch(tb), LANES), F32), pltpu.VMEM((tb, d), F32)],
        compiler_params=_params("arbitrary", "arbitrary"),
        name="peer_out",
    )(a, gg, ii, jj, vtab, x1, g2, lg.reshape(1, d), lb.reshape(1, d))


def _tile(n, want):
    t = min(n, want)
    while n % t:
        t -= 1
    return t


def kernel(x, c, w_mod, b_mod, w_in, b_f, conv_w, conv_b, conv_ln_g, conv_ln_b,
           w_attn_out, w_conv_out, w_out, ln1_g, ln1_b, peer_wq, peer_sub_keys,
           peer_u, peer_v, ln2_g, ln2_b):
    bsz, seq, d = x.shape
    depth = w_mod.shape[0]
    t = bsz * seq
    att_w = w_attn_out.shape[1]
    n_heads = b_f.shape[1]
    head_dim = att_w // n_heads
    conv_c = conv_w.shape[2]
    alpha = (2.0 * depth) ** 0.25
    assert n_heads <= LANES and LANES % head_dim == 0 and att_w % LANES == 0

    tm = _tile(seq, 512)
    xf = x.reshape(t, d)
    for l in range(depth):
        mod = _mod(c, w_mod[l], b_mod[l])
        sh1, sc1, g1, sh2, sc2, g2 = [m.reshape(bsz, 1, d) for m in jnp.split(mod, 6, axis=-1)]

        o_f = 3 * att_w
        o_c = o_f + n_heads
        o_g = o_c + 2 * conv_c
        wl = w_in[l]
        wqkv = wl[:, :o_f].astype(BF16)
        wf = jnp.pad(wl[:, o_f:o_c], ((0, 0), (0, LANES - n_heads))).astype(BF16)
        wc = wl[:, o_c:o_g].astype(BF16)
        wg = wl[:, o_g:].astype(BF16)
        qkv, f, h, gates = _inproj(xf, sh1, sc1, wqkv, wf, wc, wg, seq=seq, att_w=att_w,
                                   conv_c=conv_c, head_dim=head_dim, tm=tm)

        bf_row = jnp.pad(b_f[l], (0, LANES - n_heads)).reshape(1, LANES)
        fc = _fcum(f.reshape(bsz, seq, LANES), bf_row, tc=_tile(seq, 256))
        fh = fc[:, :, :n_heads].transpose(0, 2, 1)
        attn = _attention(qkv, fh[:, :, :, None], fh[:, :, None, :], bsz=bsz, seq=seq,
                          att_w=att_w, head_dim=head_dim, tq=tm, tk=tm)

        hc = _conv(h, conv_w[l], conv_b[l], conv_ln_g[l], conv_ln_b[l], seq=seq, ts=tm)

        x1, u2, qp = _merge(attn, hc, gates, xf, g1, sh2, sc2,
                            w_attn_out[l].astype(BF16), w_conv_out[l].astype(BF16),
                            w_out[l].astype(BF16), ln1_g[l], ln1_b[l],
                            peer_wq[l].astype(BF16), seq=seq, alpha=alpha, tm=tm)

        ii_t, jj_t, gg_t = _retrieve(qp, peer_sub_keys[l].astype(BF16), tl=tm)
        ii, jj, gg = ii_t.T, jj_t.T, gg_t.T
        a = _peer_act(u2, peer_u[l].astype(BF16), ii, jj, tb=tm, groups=16)
        xf = _peer_out(a, gg, ii, jj, peer_v[l].astype(BF16), x1, g2, ln2_g[l], ln2_b[l],
                       seq=seq, alpha=alpha, tb=_tile(seq, 256), groups=8)
    return xf.reshape(bsz, seq, d)
```

```python
import functools
import math

import jax
import jax.numpy as jnp
import numpy as np
from jax import lax
from jax.experimental import pallas as pl
from jax.experimental.pallas import tpu as pltpu

F32 = jnp.float32
BF16 = jnp.bfloat16
I32 = jnp.int32

LN_EPS = 1e-5
PEER_TOPK = 16
LANES = 128
VMEM_LIMIT_BYTES = 56 * 1024 * 1024
NEG_INF = float("-inf")
LOG2_E = math.log2(math.e)


def _params(*semantics):
    return pltpu.CompilerParams(dimension_semantics=semantics, vmem_limit_bytes=VMEM_LIMIT_BYTES)


def _normalize(x):
    mu = jnp.mean(x, axis=-1, keepdims=True)
    xc = x - mu
    var = jnp.mean(xc * xc, axis=-1, keepdims=True)
    return xc * lax.rsqrt(var + LN_EPS)


def _resident(shape):
    nd = len(shape)
    return pl.BlockSpec(shape, lambda *_: (0,) * nd)


def _mod_kernel(c_ref, w_ref, b_ref, o_ref):
    c = c_ref[...]
    cs = c * jax.nn.sigmoid(c)
    o_ref[...] = jnp.dot(cs, w_ref[...], preferred_element_type=F32,
                         precision=lax.Precision.HIGHEST) + b_ref[...]


def _mod(c, w, b):
    bsz, d = c.shape
    n = w.shape[1]
    rows = -(-bsz // 8) * 8
    tn = math.gcd(n, 1536)
    cp = jnp.pad(c, ((0, rows - bsz), (0, 0)))
    out = pl.pallas_call(
        _mod_kernel,
        grid=(n // tn,),
        in_specs=[pl.BlockSpec((rows, d), lambda j: (0, 0)),
                  pl.BlockSpec((d, tn), lambda j: (0, j)),
                  pl.BlockSpec((1, tn), lambda j: (0, j))],
        out_specs=pl.BlockSpec((rows, tn), lambda j: (0, j)),
        out_shape=jax.ShapeDtypeStruct((rows, n), F32),
        compiler_params=_params("arbitrary"),
        name="mod",
    )(cp, w, b.reshape(1, n))
    return out[:bsz]


def _inproj_kernel(x_ref, sh_ref, sc_ref, wqkv_ref, wf_ref, wc_ref, wg_ref,
                   qkv_ref, f_ref, h_ref, g_ref, *, att_w, conv_c, qscale):
    y = _normalize(x_ref[...])
    u = (y * (1.0 + sc_ref[...]) + sh_ref[...]).astype(BF16)
    qkv = jnp.dot(u, wqkv_ref[...], preferred_element_type=F32)
    qkv_ref[:, :att_w] = (qkv[:, :att_w] * qscale).astype(BF16)
    qkv_ref[:, att_w:] = qkv[:, att_w:].astype(BF16)
    f_ref[...] = jnp.dot(u, wf_ref[...], preferred_element_type=F32)
    cab = jnp.dot(u, wc_ref[...], preferred_element_type=F32)
    h_ref[...] = cab[:, :conv_c] * jax.nn.sigmoid(cab[:, conv_c:])
    g_ref[...] = jax.nn.sigmoid(jnp.dot(u, wg_ref[...], preferred_element_type=F32))


def _inproj(x2, sh, sc, wqkv, wf, wc, wg, *, seq, att_w, conv_c, head_dim, tm):
    t, d = x2.shape
    tpb = seq // tm
    row = lambda i: (i, 0)
    per_batch = pl.BlockSpec((None, 1, d), lambda i: (i // tpb, 0, 0))
    kern = functools.partial(_inproj_kernel, att_w=att_w, conv_c=conv_c,
                             qscale=LOG2_E / math.sqrt(head_dim))
    return pl.pallas_call(
        kern,
        grid=(t // tm,),
        in_specs=[pl.BlockSpec((tm, d), row), per_batch, per_batch,
                  _resident(wqkv.shape), _resident(wf.shape), _resident(wc.shape),
                  _resident(wg.shape)],
        out_specs=[pl.BlockSpec((tm, 3 * att_w), row), pl.BlockSpec((tm, LANES), row),
                   pl.BlockSpec((tm, conv_c), row), pl.BlockSpec((tm, 2 * d), row)],
        out_shape=[jax.ShapeDtypeStruct((t, 3 * att_w), BF16),
                   jax.ShapeDtypeStruct((t, LANES), F32),
                   jax.ShapeDtypeStruct((t, conv_c), F32),
                   jax.ShapeDtypeStruct((t, 2 * d), F32)],
        compiler_params=_params("arbitrary"),
        name="inproj",
    )(x2, sh, sc, wqkv, wf, wc, wg)


def _fcum_kernel(f_ref, bf_ref, hi_ref, mid_ref, lo_ref, carry_ref, *, tc):
    @pl.when(pl.program_id(1) == 0)
    def _():
        carry_ref[...] = jnp.zeros(carry_ref.shape, F32)

    z = f_ref[...] + bf_ref[...]
    logf = jnp.minimum(z, 0.0) - jnp.log1p(jnp.exp(-jnp.abs(z)))
    r = lax.broadcasted_iota(I32, (tc, tc), 0)
    c = lax.broadcasted_iota(I32, (tc, tc), 1)
    tri = jnp.where(c <= r, 1.0, 0.0).astype(F32)
    cs = jnp.dot(tri, logf, preferred_element_type=F32,
                 precision=lax.Precision.HIGHEST) + carry_ref[...]
    carry_ref[...] = cs[tc - 1:tc, :]
    rem = cs * LOG2_E
    for part_ref in (hi_ref, mid_ref, lo_ref):
        part = rem.astype(BF16)
        part_ref[...] = part
        rem = rem - part.astype(F32)


def _fcum(f3, bf_row, *, tc):
    bsz, seq, w = f3.shape
    blk = pl.BlockSpec((None, tc, w), lambda b, j: (b, j, 0))
    part = jax.ShapeDtypeStruct((bsz, seq, w), BF16)
    return pl.pallas_call(
        functools.partial(_fcum_kernel, tc=tc),
        grid=(bsz, seq // tc),
        in_specs=[blk, pl.BlockSpec((1, w), lambda b, j: (0, 0))],
        out_specs=[blk, blk, blk],
        out_shape=[part, part, part],
        scratch_shapes=[pltpu.VMEM((1, w), F32)],
        compiler_params=_params("arbitrary", "arbitrary"),
        name="fcum",
    )(f3, bf_row)


ATTN_MASKED = 1
ATTN_LAST = 2


def _attn_tables(nq, tq, tk):
    qi_l, ki_l, fl_l = [], [], []
    for qi in range(nq):
        last = (qi * tq + tq - 1) // tk
        for ki in range(last + 1):
            masked = ki * tk + tk - 1 > qi * tq
            qi_l.append(qi)
            ki_l.append(ki)
            fl_l.append((ATTN_MASKED if masked else 0) | (ATTN_LAST if ki == last else 0))
    as_i32 = lambda v: jnp.asarray(np.asarray(v, np.int32))
    return as_i32(qi_l), as_i32(ki_l), as_i32(fl_l)


def _attn_kernel(qi_tab, ki_tab, fl_tab, q_ref, k_ref, v_ref, o_ref,
                 m_sc, l_sc, acc_sc, *, tq, tk, dh, heads):
    p = pl.program_id(2)
    qi = qi_tab[p]
    ki = ki_tab[p]
    fl = fl_tab[p]

    @pl.when(ki == 0)
    def _():
        m_sc[...] = jnp.full(m_sc.shape, NEG_INF, F32)
        l_sc[...] = jnp.zeros(l_sc.shape, F32)
        acc_sc[...] = jnp.zeros(acc_sc.shape, F32)

    head_of_lane = lax.broadcasted_iota(I32, (tq, LANES), 1) // dh

    def per_head(vals):
        out = vals[0]
        for hh in range(1, heads):
            out = jnp.where(head_of_lane == hh, vals[hh], out)
        return out

    def step(masked):
        if masked:
            row = qi * tq + lax.broadcasted_iota(I32, (tq, tk), 0)
            col = ki * tk + lax.broadcasted_iota(I32, (tq, tk), 1)
            keep = col <= row
        alphas, pvs = [], []
        for hh in range(heads):
            sl = slice(hh * LANES, (hh + 1) * LANES)
            s = lax.dot_general(q_ref[:, sl], k_ref[:, sl], (((1,), (1,)), ((), ())),
                                preferred_element_type=F32)
            if masked:
                s = jnp.where(keep, s, NEG_INF)
            m_prev = m_sc[hh]
            m_new = jnp.maximum(m_prev, jnp.max(s, axis=-1, keepdims=True))
            alpha = jnp.exp2(m_prev - m_new)
            pr = jnp.exp2(s - jnp.concatenate([m_new] * (tk // LANES), axis=1))
            l_sc[hh] = alpha * l_sc[hh] + jnp.sum(pr, axis=-1, keepdims=True)
            m_sc[hh] = m_new
            alphas.append(alpha)
            pvs.append(jnp.dot(pr.astype(BF16), v_ref[...], preferred_element_type=F32))
        acc_sc[...] = per_head(alphas) * acc_sc[...] + per_head(pvs)

    pl.when((fl & ATTN_MASKED) != 0)(lambda: step(True))
    pl.when((fl & ATTN_MASKED) == 0)(lambda: step(False))

    @pl.when((fl & ATTN_LAST) != 0)
    def _():
        o_ref[...] = (acc_sc[...] / per_head([l_sc[hh] for hh in range(heads)])).astype(o_ref.dtype)


def _attention(q_aug, k_aug, qkv, *, bsz, seq, att_w, head_dim, tq, tk):
    heads = LANES // head_dim
    ncol = att_w // LANES
    nq, nk = seq // tq, seq // tk
    assert tk % LANES == 0
    qi_tab, ki_tab, fl_tab = _attn_tables(nq, tq, tk)
    kern = functools.partial(_attn_kernel, tq=tq, tk=tk, dh=head_dim, heads=heads)
    grid_spec = pltpu.PrefetchScalarGridSpec(
        num_scalar_prefetch=3,
        grid=(bsz, ncol, int(qi_tab.shape[0])),
        in_specs=[
            pl.BlockSpec((tq, heads * LANES), lambda b, c, p, qt, kt, ft: (b * nq + qt[p], c)),
            pl.BlockSpec((tk, heads * LANES), lambda b, c, p, qt, kt, ft: (b * nk + kt[p], c)),
            pl.BlockSpec((tk, LANES), lambda b, c, p, qt, kt, ft: (b * nk + kt[p], 2 * ncol + c)),
        ],
        out_specs=pl.BlockSpec((tq, LANES), lambda b, c, p, qt, kt, ft: (b * nq + qt[p], c)),
        scratch_shapes=[pltpu.VMEM((heads, tq, LANES), F32), pltpu.VMEM((heads, tq, LANES), F32),
                        pltpu.VMEM((tq, LANES), F32)],
    )
    return pl.pallas_call(
        kern,
        grid_spec=grid_spec,
        out_shape=jax.ShapeDtypeStruct((bsz * seq, att_w), BF16),
        compiler_params=_params("arbitrary", "arbitrary", "arbitrary"),
        name="attn",
    )(qi_tab, ki_tab, fl_tab, q_aug, k_aug, qkv)


def _augment_qk(qkv, parts, *, n_heads, head_dim, att_w):
    t = qkv.shape[0]
    nparts = len(parts)
    pad = LANES - head_dim - 2 * nparts
    assert pad >= 0
    fparts = jnp.stack([p.reshape(t, LANES)[:, :n_heads] for p in parts], axis=-1)
    ones = jnp.ones_like(fparts)
    zeros = jnp.zeros((t, n_heads, pad), BF16)
    q = qkv[:, :att_w].reshape(t, n_heads, head_dim)
    k = qkv[:, att_w:2 * att_w].reshape(t, n_heads, head_dim)
    q_aug = jnp.concatenate([q, fparts, -ones, zeros], axis=-1).reshape(t, n_heads * LANES)
    k_aug = jnp.concatenate([k, ones, fparts, zeros], axis=-1).reshape(t, n_heads * LANES)
    return q_aug, k_aug


CONV_HALO = 32
CONV_CHUNK = 64


def _conv_kernel(h_ref, halo_ref, w_ref, b_ref, g_ref, beta_ref, o_ref, pad_sc, *, ts, taps, tpb):
    first = pl.program_id(0) % tpb == 0
    pad_sc[0:CONV_HALO, :] = jnp.where(first, 0.0, halo_ref[...])
    pad_sc[CONV_HALO:, :] = h_ref[...]
    lead = CONV_HALO - (taps - 1)
    for c0 in range(0, ts, CONV_CHUNK):
        acc = jnp.broadcast_to(b_ref[...], (CONV_CHUNK, b_ref.shape[1]))
        for k in range(taps):
            acc = acc + w_ref[k:k + 1, :] * pad_sc[c0 + lead + k:c0 + lead + k + CONV_CHUNK, :]
        y = _normalize(acc) * g_ref[...] + beta_ref[...]
        o_ref[c0:c0 + CONV_CHUNK, :] = (y * jax.nn.sigmoid(y)).astype(o_ref.dtype)


def _conv(h, w, b, g, beta, *, seq, ts):
    t, c = h.shape
    taps = w.shape[0]
    assert taps - 1 <= CONV_HALO and ts % CONV_CHUNK == 0 and ts % CONV_HALO == 0
    tpb = seq // ts
    hb = ts // CONV_HALO
    vec = lambda v: v.reshape(1, c)
    return pl.pallas_call(
        functools.partial(_conv_kernel, ts=ts, taps=taps, tpb=tpb),
        grid=(t // ts,),
        in_specs=[pl.BlockSpec((ts, c), lambda i: (i, 0)),
                  pl.BlockSpec((CONV_HALO, c), lambda i: (jnp.maximum(i * hb - 1, 0), 0)),
                  _resident((taps, c)), _resident((1, c)), _resident((1, c)), _resident((1, c))],
        out_specs=pl.BlockSpec((ts, c), lambda i: (i, 0)),
        out_shape=jax.ShapeDtypeStruct((t, c), BF16),
        scratch_shapes=[pltpu.VMEM((ts + CONV_HALO, c), F32)],
        compiler_params=_params("arbitrary"),
        name="conv",
    )(h, h, w, vec(b), vec(g), vec(beta))


def _merge_kernel(attn_ref, hc_ref, g_ref, x_ref, g1_ref, sh2_ref, sc2_ref, wa_ref, wc_ref,
                  wo_ref, lg_ref, lb_ref, wq_ref, x1_ref, u2_ref, qp_ref, *, alpha):
    d = x_ref.shape[1]
    ya = jnp.dot(attn_ref[...], wa_ref[...], preferred_element_type=F32)
    yb = jnp.dot(hc_ref[...], wc_ref[...], preferred_element_type=F32)
    merged = (g_ref[:, :d] * ya + g_ref[:, d:] * yb).astype(BF16)
    out = jnp.dot(merged, wo_ref[...], preferred_element_type=F32)
    x1 = _normalize(alpha * x_ref[...] + (1.0 + g1_ref[...]) * out) * lg_ref[...] + lb_ref[...]
    x1_ref[...] = x1
    u2 = (_normalize(x1) * (1.0 + sc2_ref[...]) + sh2_ref[...]).astype(BF16)
    u2_ref[...] = u2
    qp_ref[...] = jnp.dot(u2, wq_ref[...], preferred_element_type=F32)


def _merge(attn, hc, gates, x2, g1, sh2, sc2, wa, wc, wo, lg, lb, wq, *, seq, alpha, tm):
    t, d = x2.shape
    tpb = seq // tm
    row = lambda i: (i, 0)
    per_batch = pl.BlockSpec((None, 1, d), lambda i: (i // tpb, 0, 0))
    nq = wq.shape[1]
    return pl.pallas_call(
        functools.partial(_merge_kernel, alpha=alpha),
        grid=(t // tm,),
        in_specs=[pl.BlockSpec((tm, attn.shape[1]), row), pl.BlockSpec((tm, hc.shape[1]), row),
                  pl.BlockSpec((tm, 2 * d), row), pl.BlockSpec((tm, d), row),
                  per_batch, per_batch, per_batch,
                  _resident(wa.shape), _resident(wc.shape), _resident(wo.shape),
                  _resident((1, d)), _resident((1, d)), _resident(wq.shape)],
        out_specs=[pl.BlockSpec((tm, d), row), pl.BlockSpec((tm, d), row),
                   pl.BlockSpec((tm, nq), row)],
        out_shape=[jax.ShapeDtypeStruct((t, d), F32), jax.ShapeDtypeStruct((t, d), BF16),
                   jax.ShapeDtypeStruct((t, nq), F32)],
        compiler_params=_params("arbitrary"),
        name="merge",
    )(attn, hc, gates, x2, g1, sh2, sc2, wa, wc, wo, lg.reshape(1, d), lb.reshape(1, d), wq)


def _staircase(k):
    return [(a, b) for a in range(k) for b in range(k) if (a + 1) * (b + 1) <= k]


def _extract_topk(s, ids, k):
    big = jnp.iinfo(jnp.int32).max
    vals, idxs = [], []
    for _ in range(k):
        m = jnp.max(s, axis=0, keepdims=True)
        i = jnp.min(jnp.where(s == m, ids, big), axis=0, keepdims=True)
        s = jnp.where(ids == i, NEG_INF, s)
        vals.append(m)
        idxs.append(i)
    return jnp.concatenate(vals, axis=0), jnp.concatenate(idxs, axis=0)


def _retrieve_kernel(qp_ref, keys_ref, flat_ref, ii_ref, jj_ref, gg_ref, *, dhalf, nkeys, topk):
    tl = qp_ref.shape[0]
    qb = qp_ref[...].astype(BF16)
    key_ids = lax.broadcasted_iota(I32, (nkeys, tl), 0)
    sv, si = [], []
    for half in range(2):
        s = lax.dot_general(keys_ref[half], qb[:, half * dhalf:(half + 1) * dhalf],
                            (((1,), (1,)), ((), ())), preferred_element_type=F32)
        v, i = _extract_topk(s, key_ids, topk)
        sv.append(v)
        si.append(i)

    pairs = _staircase(topk)
    npad = flat_ref.shape[0]
    cand = [sv[0][a:a + 1, :] + sv[1][b:b + 1, :] for a, b in pairs]
    cand += [jnp.full((1, tl), NEG_INF, F32)] * (npad - len(pairs))
    cand = jnp.concatenate(cand, axis=0)
    cv, ci = _extract_topk(cand, flat_ref[...], topk)

    rank = lax.broadcasted_iota(I32, (topk, tl), 0)
    ia = lax.shift_right_logical(ci, int(math.log2(topk)))
    ib = ci & (topk - 1)
    ii_rows, jj_rows = [], []
    for r in range(topk):
        ii_rows.append(jnp.sum(jnp.where(rank == ia[r:r + 1, :], si[0], 0), axis=0, keepdims=True))
        jj_rows.append(jnp.sum(jnp.where(rank == ib[r:r + 1, :], si[1], 0), axis=0, keepdims=True))
    ii_ref[...] = jnp.concatenate(ii_rows, axis=0)
    jj_ref[...] = jnp.concatenate(jj_rows, axis=0)
    e = jnp.exp(cv - cv[0:1, :])
    gg_ref[...] = e / jnp.sum(e, axis=0, keepdims=True)


def _retrieve(qp, keys, *, tl):
    t, nq = qp.shape
    _, nkeys, dhalf = keys.shape
    heads = nq // (2 * dhalf)
    assert 2 * dhalf == LANES and PEER_TOPK & (PEER_TOPK - 1) == 0
    pairs = _staircase(PEER_TOPK)
    npad = -(-len(pairs) // 8) * 8
    flat = np.full((npad, tl), np.iinfo(np.int32).max - 1, np.int32)
    flat[:len(pairs), :] = np.asarray([a * PEER_TOPK + b for a, b in pairs], np.int32)[:, None]
    kern = functools.partial(_retrieve_kernel, dhalf=dhalf, nkeys=nkeys, topk=PEER_TOPK)
    out = jax.ShapeDtypeStruct((heads * PEER_TOPK, t), I32)
    blk = pl.BlockSpec((PEER_TOPK, tl), lambda i, h: (h, i))
    return pl.pallas_call(
        kern,
        grid=(t // tl, heads),
        in_specs=[pl.BlockSpec((tl, LANES), lambda i, h: (i, h)), _resident(keys.shape),
                  _resident(flat.shape)],
        out_specs=[blk, blk, blk],
        out_shape=[out, out, jax.ShapeDtypeStruct((heads * PEER_TOPK, t), F32)],
        compiler_params=_params("arbitrary", "arbitrary"),
        name="retrieve",
    )(qp, keys, jnp.asarray(flat))


def _gelu_tanh(a):
    return 0.5 * a * (1.0 + jnp.tanh(math.sqrt(2.0 / math.pi) * (a + 0.044715 * a * a * a)))


def _peer_act_kernel(x_ref, u_ref, ii_ref, jj_ref, a_ref, *, groups):
    c = pl.program_id(1)

    @pl.when(c == 0)
    def _():
        a_ref[...] = jnp.zeros(a_ref.shape, F32)

    a_all = lax.dot_general(x_ref[...], u_ref[...], (((1,), (1,)), ((), ())),
                            preferred_element_type=F32)
    ii = ii_ref[...]
    jj = jj_ref[...]
    acc = a_ref[...]
    for g in range(groups):
        picked = jnp.take_along_axis(a_all[:, g * LANES:(g + 1) * LANES], jj, axis=1)
        acc = jnp.where(ii == c * groups + g, picked, acc)
    a_ref[...] = acc


def _peer_act(u2, utab, ii, jj, *, tb, groups):
    t, d = u2.shape
    ne = utab.shape[0]
    ce = groups * LANES
    tok = lambda i, c: (i, 0)
    return pl.pallas_call(
        functools.partial(_peer_act_kernel, groups=groups),
        grid=(t // tb, ne // ce),
        in_specs=[pl.BlockSpec((tb, d), tok), pl.BlockSpec((ce, d), lambda i, c: (c, 0)),
                  pl.BlockSpec((tb, LANES), tok), pl.BlockSpec((tb, LANES), tok)],
        out_specs=pl.BlockSpec((tb, LANES), tok),
        out_shape=jax.ShapeDtypeStruct((t, LANES), F32),
        compiler_params=_params("arbitrary", "arbitrary"),
        name="peer_act",
    )(u2, utab, ii, jj)


SCATTER_UNROLL = 16


def _w_pitch(tb):
    p = -(-tb // 8)
    return 8 * (p if p % 2 else p + 1)


def _peer_out_kernel(a_ref, gg_ref, ii_ref, jj_ref, v_ref, x1_ref, g2_ref, lg_ref, lb_ref,
                     o_ref, w_sc, acc_sc, *, tb, groups, nchunks, alpha):
    c = pl.program_id(1)
    pitch = _w_pitch(tb)

    @pl.when(c == 0)
    def _():
        acc_sc[...] = jnp.zeros(acc_sc.shape, F32)
        sub = lax.broadcasted_iota(I32, (LANES, LANES), 0)

        def scatter(tok, carry):
            wrow = _gelu_tanh(a_ref[pl.ds(tok, 1), :]) * gg_ref[pl.ds(tok, 1), :]
            lhs = jnp.where(sub == ii_ref[pl.ds(tok, 1), :], wrow, 0.0).astype(BF16)
            rhs = jnp.where(sub == jj_ref[pl.ds(tok, 1), :], 1.0, 0.0).astype(BF16)
            w_sc[pl.ds(tok, LANES, stride=pitch), :] = lax.dot_general(
                lhs, rhs, (((1,), (1,)), ((), ())), preferred_element_type=F32)
            return carry

        lax.fori_loop(0, tb, scatter, 0, unroll=SCATTER_UNROLL)

    parts = [w_sc[pl.ds(pl.multiple_of((c * groups + g) * pitch, 8), tb), :].astype(BF16)
             for g in range(groups)]
    acc_sc[...] += jnp.dot(jnp.concatenate(parts, axis=1), v_ref[...],
                           preferred_element_type=F32)

    @pl.when(c == nchunks - 1)
    def _():
        r = alpha * x1_ref[...] + (1.0 + g2_ref[...]) * acc_sc[...]
        o_ref[...] = _normalize(r) * lg_ref[...] + lb_ref[...]


def _peer_out(a, gg, ii, jj, vtab, x1, g2, lg, lb, *, seq, alpha, tb, groups):
    t, d = x1.shape
    ne = vtab.shape[0]
    assert ne == LANES * LANES
    ce = groups * LANES
    nchunks = ne // ce
    tpb = seq // tb
    tok = lambda i, c: (i, 0)
    kern = functools.partial(_peer_out_kernel, tb=tb, groups=groups, nchunks=nchunks, alpha=alpha)
    return pl.pallas_call(
        kern,
        grid=(t // tb, nchunks),
        in_specs=[pl.BlockSpec((tb, LANES), tok), pl.BlockSpec((tb, LANES), tok),
                  pl.BlockSpec((tb, LANES), tok), pl.BlockSpec((tb, LANES), tok),
                  pl.BlockSpec((ce, d), lambda i, c: (c, 0)),
                  pl.BlockSpec((tb, d), tok),
                  pl.BlockSpec((None, 1, d), lambda i, c: (i // tpb, 0, 0)),
                  _resident((1, d)), _resident((1, d))],
        out_specs=pl.BlockSpec((tb, d), tok),
        out_shape=jax.ShapeDtypeStruct((t, d), F32),
        scratch_shapes=[pltpu.VMEM((LANES * _w_pitch(tb), LANES), F32), pltpu.VMEM((tb, d), F32)],
        compiler_params=_params("arbitrary", "arbitrary"),
        name="peer_out",
    )(a, gg, ii, jj, vtab, x1, g2, lg.reshape(1, d), lb.reshape(1, d))


def _tile(n, want):
    t = min(n, want)
    while n % t:
        t -= 1
    return t


def kernel(x, c, w_mod, b_mod, w_in, b_f, conv_w, conv_b, conv_ln_g, conv_ln_b,
           w_attn_out, w_conv_out, w_out, ln1_g, ln1_b, peer_wq, peer_sub_keys,
           peer_u, peer_v, ln2_g, ln2_b):
    bsz, seq, d = x.shape
    depth = w_mod.shape[0]
    t = bsz * seq
    att_w = w_attn_out.shape[1]
    n_heads = b_f.shape[1]
    head_dim = att_w // n_heads
    conv_c = conv_w.shape[2]
    alpha = (2.0 * depth) ** 0.25
    assert n_heads <= LANES and LANES % head_dim == 0 and att_w % LANES == 0

    tm = _tile(seq, 512)
    xf = x.reshape(t, d)
    for l in range(depth):
        mod = _mod(c, w_mod[l], b_mod[l])
        sh1, sc1, g1, sh2, sc2, g2 = [m.reshape(bsz, 1, d) for m in jnp.split(mod, 6, axis=-1)]

        o_f = 3 * att_w
        o_c = o_f + n_heads
        o_g = o_c + 2 * conv_c
        wl = w_in[l]
        wqkv = wl[:, :o_f].astype(BF16)
        wf = jnp.pad(wl[:, o_f:o_c], ((0, 0), (0, LANES - n_heads))).astype(BF16)
        wc = wl[:, o_c:o_g].astype(BF16)
        wg = wl[:, o_g:].astype(BF16)
        qkv, f, h, gates = _inproj(xf, sh1, sc1, wqkv, wf, wc, wg, seq=seq, att_w=att_w,
                                   conv_c=conv_c, head_dim=head_dim, tm=tm)

        bf_row = jnp.pad(b_f[l], (0, LANES - n_heads)).reshape(1, LANES)
        fparts = _fcum(f.reshape(bsz, seq, LANES), bf_row, tc=_tile(seq, 256))
        q_aug, k_aug = _augment_qk(qkv, fparts, n_heads=n_heads, head_dim=head_dim, att_w=att_w)
        attn = _attention(q_aug, k_aug, qkv, bsz=bsz, seq=seq, att_w=att_w, head_dim=head_dim,
                          tq=tm, tk=tm)

        hc = _conv(h, conv_w[l], conv_b[l], conv_ln_g[l], conv_ln_b[l], seq=seq, ts=tm)

        x1, u2, qp = _merge(attn, hc, gates, xf, g1, sh2, sc2,
                            w_attn_out[l].astype(BF16), w_conv_out[l].astype(BF16),
                            w_out[l].astype(BF16), ln1_g[l], ln1_b[l],
                            peer_wq[l].astype(BF16), seq=seq, alpha=alpha, tm=tm)

        ii_t, jj_t, gg_t = _retrieve(qp, peer_sub_keys[l].astype(BF16), tl=tm)
        ii, jj, gg = ii_t.T, jj_t.T, gg_t.T
        a = _peer_act(u2, peer_u[l].astype(BF16), ii, jj, tb=tm, groups=16)
        xf = _peer_out(a, gg, ii, jj, peer_v[l].astype(BF16), x1, g2, ln2_g[l], ln2_b[l],
                       seq=seq, alpha=alpha, tb=_tile(seq, 256), groups=8)
    return xf.reshape(bsz, seq, d)
```

```python
import functools
import math

import jax
import jax.numpy as jnp
import numpy as np
from jax import lax
from jax.experimental import pallas as pl
from jax.experimental.pallas import tpu as pltpu

F32 = jnp.float32
BF16 = jnp.bfloat16
I32 = jnp.int32

LN_EPS = 1e-5
PEER_TOPK = 16
LANES = 128
VMEM_LIMIT_BYTES = 56 * 1024 * 1024
NEG_INF = float("-inf")
LOG2_E = math.log2(math.e)


def _params(*semantics):
    return pltpu.CompilerParams(dimension_semantics=semantics, vmem_limit_bytes=VMEM_LIMIT_BYTES)


def _normalize(x):
    mu = jnp.mean(x, axis=-1, keepdims=True)
    xc = x - mu
    var = jnp.mean(xc * xc, axis=-1, keepdims=True)
    return xc * lax.rsqrt(var + LN_EPS)


def _resident(shape):
    nd = len(shape)
    return pl.BlockSpec(shape, lambda *_: (0,) * nd)


def _mod_kernel(c_ref, w_ref, b_ref, o_ref):
    c = c_ref[...]
    cs = c * jax.nn.sigmoid(c)
    o_ref[...] = jnp.dot(cs, w_ref[...], preferred_element_type=F32,
                         precision=lax.Precision.HIGHEST) + b_ref[...]


def _mod(c, w, b):
    bsz, d = c.shape
    n = w.shape[1]
    rows = -(-bsz // 8) * 8
    tn = math.gcd(n, 1536)
    cp = jnp.pad(c, ((0, rows - bsz), (0, 0)))
    out = pl.pallas_call(
        _mod_kernel,
        grid=(n // tn,),
        in_specs=[pl.BlockSpec((rows, d), lambda j: (0, 0)),
                  pl.BlockSpec((d, tn), lambda j: (0, j)),
                  pl.BlockSpec((1, tn), lambda j: (0, j))],
        out_specs=pl.BlockSpec((rows, tn), lambda j: (0, j)),
        out_shape=jax.ShapeDtypeStruct((rows, n), F32),
        compiler_params=_params("arbitrary"),
        name="mod",
    )(cp, w, b.reshape(1, n))
    return out[:bsz]


def _inproj_kernel(x_ref, sh_ref, sc_ref, wqkv_ref, wf_ref, wc_ref, wg_ref,
                   qkv_ref, f_ref, h_ref, g_ref, *, att_w, conv_c, qscale):
    y = _normalize(x_ref[...])
    u = (y * (1.0 + sc_ref[...]) + sh_ref[...]).astype(BF16)
    qkv = jnp.dot(u, wqkv_ref[...], preferred_element_type=F32)
    qkv_ref[:, :att_w] = (qkv[:, :att_w] * qscale).astype(BF16)
    qkv_ref[:, att_w:] = qkv[:, att_w:].astype(BF16)
    f_ref[...] = jnp.dot(u, wf_ref[...], preferred_element_type=F32)
    cab = jnp.dot(u, wc_ref[...], preferred_element_type=F32)
    h_ref[...] = cab[:, :conv_c] * jax.nn.sigmoid(cab[:, conv_c:])
    g_ref[...] = jax.nn.sigmoid(jnp.dot(u, wg_ref[...], preferred_element_type=F32))


def _inproj(x2, sh, sc, wqkv, wf, wc, wg, *, seq, att_w, conv_c, head_dim, tm):
    t, d = x2.shape
    tpb = seq // tm
    row = lambda i: (i, 0)
    per_batch = pl.BlockSpec((None, 1, d), lambda i: (i // tpb, 0, 0))
    kern = functools.partial(_inproj_kernel, att_w=att_w, conv_c=conv_c,
                             qscale=LOG2_E / math.sqrt(head_dim))
    return pl.pallas_call(
        kern,
        grid=(t // tm,),
        in_specs=[pl.BlockSpec((tm, d), row), per_batch, per_batch,
                  _resident(wqkv.shape), _resident(wf.shape), _resident(wc.shape),
                  _resident(wg.shape)],
        out_specs=[pl.BlockSpec((tm, 3 * att_w), row), pl.BlockSpec((tm, LANES), row),
                   pl.BlockSpec((tm, conv_c), row), pl.BlockSpec((tm, 2 * d), row)],
        out_shape=[jax.ShapeDtypeStruct((t, 3 * att_w), BF16),
                   jax.ShapeDtypeStruct((t, LANES), F32),
                   jax.ShapeDtypeStruct((t, conv_c), F32),
                   jax.ShapeDtypeStruct((t, 2 * d), F32)],
        compiler_params=_params("arbitrary"),
        name="inproj",
    )(x2, sh, sc, wqkv, wf, wc, wg)


BIAS_TERMS = 3


def _fcum_kernel(f_ref, bf_ref, qb_ref, kb_ref, carry_ref, *, tc, n_heads):
    @pl.when(pl.program_id(1) == 0)
    def _():
        carry_ref[...] = jnp.zeros(carry_ref.shape, F32)

    z = f_ref[...] + bf_ref[...]
    logf = jnp.minimum(z, 0.0) - jnp.log1p(jnp.exp(-jnp.abs(z)))
    r = lax.broadcasted_iota(I32, (tc, tc), 0)
    c = lax.broadcasted_iota(I32, (tc, tc), 1)
    tri = jnp.where(c <= r, 1.0, 0.0).astype(F32)
    cs = jnp.dot(tri, logf, preferred_element_type=F32,
                 precision=lax.Precision.HIGHEST) + carry_ref[...]
    carry_ref[...] = cs[tc - 1:tc, :]
    rem = cs * LOG2_E
    terms = []
    for _ in range(BIAS_TERMS):
        term = rem.astype(BF16)
        terms.append(term.astype(F32))
        rem = rem - terms[-1]
    lane = lax.broadcasted_iota(I32, (tc, LANES), 1)
    for hh in range(n_heads):
        qb = jnp.where((lane >= BIAS_TERMS) & (lane < 2 * BIAS_TERMS), -1.0, 0.0)
        kb = jnp.where(lane < BIAS_TERMS, 1.0, 0.0)
        for r, term in enumerate(terms):
            col = jnp.broadcast_to(term[:, hh:hh + 1], (tc, LANES))
            qb = jnp.where(lane == r, col, qb)
            kb = jnp.where(lane == BIAS_TERMS + r, col, kb)
        qb_ref[:, hh * LANES:(hh + 1) * LANES] = qb.astype(BF16)
        kb_ref[:, hh * LANES:(hh + 1) * LANES] = kb.astype(BF16)


def _fcum(f3, bf_row, *, n_heads, tc):
    bsz, seq, w = f3.shape
    slab = pl.BlockSpec((None, tc, n_heads * LANES), lambda b, j: (b, j, 0))
    out = jax.ShapeDtypeStruct((bsz, seq, n_heads * LANES), BF16)
    return pl.pallas_call(
        functools.partial(_fcum_kernel, tc=tc, n_heads=n_heads),
        grid=(bsz, seq // tc),
        in_specs=[pl.BlockSpec((None, tc, w), lambda b, j: (b, j, 0)),
                  pl.BlockSpec((1, w), lambda b, j: (0, 0))],
        out_specs=[slab, slab],
        out_shape=[out, out],
        scratch_shapes=[pltpu.VMEM((1, w), F32)],
        compiler_params=_params("arbitrary", "arbitrary"),
        name="fcum",
    )(f3, bf_row)


ATTN_MASKED = 1
ATTN_LAST = 2


def _attn_tables(nq, tq, tk):
    qi_l, ki_l, fl_l = [], [], []
    for qi in range(nq):
        last = (qi * tq + tq - 1) // tk
        for ki in range(last + 1):
            masked = ki * tk + tk - 1 > qi * tq
            qi_l.append(qi)
            ki_l.append(ki)
            fl_l.append((ATTN_MASKED if masked else 0) | (ATTN_LAST if ki == last else 0))
    as_i32 = lambda v: jnp.asarray(np.asarray(v, np.int32))
    return as_i32(qi_l), as_i32(ki_l), as_i32(fl_l)


def _attn_kernel(qi_tab, ki_tab, fl_tab, q_ref, qb_ref, k_ref, kb_ref, v_ref, o_ref,
                 qop_sc, m_sc, l_sc, acc_sc, *, tq, tk, dh, heads):
    p = pl.program_id(2)
    qi = qi_tab[p]
    ki = ki_tab[p]
    fl = fl_tab[p]
    head_of_lane = lax.broadcasted_iota(I32, (tq, LANES), 1) // dh

    @pl.when(ki == 0)
    def _():
        m_sc[...] = jnp.full(m_sc.shape, NEG_INF, F32)
        l_sc[...] = jnp.zeros(l_sc.shape, F32)
        acc_sc[...] = jnp.zeros(acc_sc.shape, F32)
        q = q_ref[...]
        for hh in range(heads):
            qop_sc[hh, :, :LANES] = jnp.where(head_of_lane == hh, q, jnp.zeros_like(q))
            qop_sc[hh, :, LANES:] = qb_ref[:, hh * LANES:(hh + 1) * LANES]

    def per_head(vals):
        out = vals[0]
        for hh in range(1, heads):
            out = jnp.where(head_of_lane == hh, vals[hh], out)
        return out

    def step(masked):
        if masked:
            row = qi * tq + lax.broadcasted_iota(I32, (tq, tk), 0)
            col = ki * tk + lax.broadcasted_iota(I32, (tq, tk), 1)
            keep = col <= row
        alphas, pvs = [], []
        k = k_ref[...]
        for hh in range(heads):
            kop = jnp.concatenate([k, kb_ref[:, hh * LANES:(hh + 1) * LANES]], axis=1)
            s = lax.dot_general(qop_sc[hh], kop, (((1,), (1,)), ((), ())),
                                preferred_element_type=F32)
            if masked:
                s = jnp.where(keep, s, NEG_INF)
            m_prev = m_sc[hh]
            m_new = jnp.maximum(m_prev, jnp.max(s, axis=-1, keepdims=True))
            alpha = jnp.exp2(m_prev - m_new)
            pr = jnp.exp2(s - jnp.concatenate([m_new] * (tk // LANES), axis=1))
            l_sc[hh] = alpha * l_sc[hh] + jnp.sum(pr, axis=-1, keepdims=True)
            m_sc[hh] = m_new
            alphas.append(alpha)
            pvs.append(jnp.dot(pr.astype(BF16), v_ref[...], preferred_element_type=F32))
        acc_sc[...] = per_head(alphas) * acc_sc[...] + per_head(pvs)

    pl.when((fl & ATTN_MASKED) != 0)(lambda: step(True))
    pl.when((fl & ATTN_MASKED) == 0)(lambda: step(False))

    @pl.when((fl & ATTN_LAST) != 0)
    def _():
        o_ref[...] = (acc_sc[...] / per_head([l_sc[hh] for hh in range(heads)])).astype(o_ref.dtype)


def _attention(qkv, qb, kb, *, bsz, seq, att_w, head_dim, tq, tk):
    heads = LANES // head_dim
    ncol = att_w // LANES
    nq, nk = seq // tq, seq // tk
    assert tk % LANES == 0
    qi_tab, ki_tab, fl_tab = _attn_tables(nq, tq, tk)
    kern = functools.partial(_attn_kernel, tq=tq, tk=tk, dh=head_dim, heads=heads)
    qrow = lambda b, c, p, qt, kt, ft: b * nq + qt[p]
    krow = lambda b, c, p, qt, kt, ft: b * nk + kt[p]
    grid_spec = pltpu.PrefetchScalarGridSpec(
        num_scalar_prefetch=3,
        grid=(bsz, ncol, int(qi_tab.shape[0])),
        in_specs=[
            pl.BlockSpec((tq, LANES), lambda b, c, *a: (qrow(b, c, *a), c)),
            pl.BlockSpec((tq, heads * LANES), lambda b, c, *a: (qrow(b, c, *a), c)),
            pl.BlockSpec((tk, LANES), lambda b, c, *a: (krow(b, c, *a), ncol + c)),
            pl.BlockSpec((tk, heads * LANES), lambda b, c, *a: (krow(b, c, *a), c)),
            pl.BlockSpec((tk, LANES), lambda b, c, *a: (krow(b, c, *a), 2 * ncol + c)),
        ],
        out_specs=pl.BlockSpec((tq, LANES), lambda b, c, *a: (qrow(b, c, *a), c)),
        scratch_shapes=[pltpu.VMEM((heads, tq, 2 * LANES), BF16),
                        pltpu.VMEM((heads, tq, LANES), F32), pltpu.VMEM((heads, tq, LANES), F32),
                        pltpu.VMEM((tq, LANES), F32)],
    )
    return pl.pallas_call(
        kern,
        grid_spec=grid_spec,
        out_shape=jax.ShapeDtypeStruct((bsz * seq, att_w), BF16),
        compiler_params=_params("arbitrary", "arbitrary", "arbitrary"),
        name="attn",
    )(qi_tab, ki_tab, fl_tab, qkv, qb, qkv, kb, qkv)


CONV_HALO = 32
CONV_CHUNK = 64


def _conv_kernel(h_ref, halo_ref, w_ref, b_ref, g_ref, beta_ref, o_ref, pad_sc, *, ts, taps, tpb):
    first = pl.program_id(0) % tpb == 0
    pad_sc[0:CONV_HALO, :] = jnp.where(first, 0.0, halo_ref[...])
    pad_sc[CONV_HALO:, :] = h_ref[...]
    lead = CONV_HALO - (taps - 1)
    for c0 in range(0, ts, CONV_CHUNK):
        acc = jnp.broadcast_to(b_ref[...], (CONV_CHUNK, b_ref.shape[1]))
        for k in range(taps):
            acc = acc + w_ref[k:k + 1, :] * pad_sc[c0 + lead + k:c0 + lead + k + CONV_CHUNK, :]
        y = _normalize(acc) * g_ref[...] + beta_ref[...]
        o_ref[c0:c0 + CONV_CHUNK, :] = (y * jax.nn.sigmoid(y)).astype(o_ref.dtype)


def _conv(h, w, b, g, beta, *, seq, ts):
    t, c = h.shape
    taps = w.shape[0]
    assert taps - 1 <= CONV_HALO and ts % CONV_CHUNK == 0 and ts % CONV_HALO == 0
    tpb = seq // ts
    hb = ts // CONV_HALO
    vec = lambda v: v.reshape(1, c)
    return pl.pallas_call(
        functools.partial(_conv_kernel, ts=ts, taps=taps, tpb=tpb),
        grid=(t // ts,),
        in_specs=[pl.BlockSpec((ts, c), lambda i: (i, 0)),
                  pl.BlockSpec((CONV_HALO, c), lambda i: (jnp.maximum(i * hb - 1, 0), 0)),
                  _resident((taps, c)), _resident((1, c)), _resident((1, c)), _resident((1, c))],
        out_specs=pl.BlockSpec((ts, c), lambda i: (i, 0)),
        out_shape=jax.ShapeDtypeStruct((t, c), BF16),
        scratch_shapes=[pltpu.VMEM((ts + CONV_HALO, c), F32)],
        compiler_params=_params("arbitrary"),
        name="conv",
    )(h, h, w, vec(b), vec(g), vec(beta))


def _merge_kernel(attn_ref, hc_ref, g_ref, x_ref, g1_ref, sh2_ref, sc2_ref, wa_ref, wc_ref,
                  wo_ref, lg_ref, lb_ref, wq_ref, x1_ref, u2_ref, qp_ref, *, alpha):
    d = x_ref.shape[1]
    ya = jnp.dot(attn_ref[...], wa_ref[...], preferred_element_type=F32)
    yb = jnp.dot(hc_ref[...], wc_ref[...], preferred_element_type=F32)
    merged = (g_ref[:, :d] * ya + g_ref[:, d:] * yb).astype(BF16)
    out = jnp.dot(merged, wo_ref[...], preferred_element_type=F32)
    x1 = _normalize(alpha * x_ref[...] + (1.0 + g1_ref[...]) * out) * lg_ref[...] + lb_ref[...]
    x1_ref[...] = x1
    u2 = (_normalize(x1) * (1.0 + sc2_ref[...]) + sh2_ref[...]).astype(BF16)
    u2_ref[...] = u2
    qp_ref[...] = jnp.dot(u2, wq_ref[...], preferred_element_type=F32)


def _merge(attn, hc, gates, x2, g1, sh2, sc2, wa, wc, wo, lg, lb, wq, *, seq, alpha, tm):
    t, d = x2.shape
    tpb = seq // tm
    row = lambda i: (i, 0)
    per_batch = pl.BlockSpec((None, 1, d), lambda i: (i // tpb, 0, 0))
    nq = wq.shape[1]
    return pl.pallas_call(
        functools.partial(_merge_kernel, alpha=alpha),
        grid=(t // tm,),
        in_specs=[pl.BlockSpec((tm, attn.shape[1]), row), pl.BlockSpec((tm, hc.shape[1]), row),
                  pl.BlockSpec((tm, 2 * d), row), pl.BlockSpec((tm, d), row),
                  per_batch, per_batch, per_batch,
                  _resident(wa.shape), _resident(wc.shape), _resident(wo.shape),
                  _resident((1, d)), _resident((1, d)), _resident(wq.shape)],
        out_specs=[pl.BlockSpec((tm, d), row), pl.BlockSpec((tm, d), row),
                   pl.BlockSpec((tm, nq), row)],
        out_shape=[jax.ShapeDtypeStruct((t, d), F32), jax.ShapeDtypeStruct((t, d), BF16),
                   jax.ShapeDtypeStruct((t, nq), F32)],
        compiler_params=_params("arbitrary"),
        name="merge",
    )(attn, hc, gates, x2, g1, sh2, sc2, wa, wc, wo, lg.reshape(1, d), lb.reshape(1, d), wq)


def _staircase(k):
    return [(a, b) for a in range(k) for b in range(k) if (a + 1) * (b + 1) <= k]


def _extract_topk(s, ids, k):
    big = jnp.iinfo(jnp.int32).max
    vals, idxs = [], []
    for _ in range(k):
        m = jnp.max(s, axis=0, keepdims=True)
        i = jnp.min(jnp.where(s == m, ids, big), axis=0, keepdims=True)
        s = jnp.where(ids == i, NEG_INF, s)
        vals.append(m)
        idxs.append(i)
    return jnp.concatenate(vals, axis=0), jnp.concatenate(idxs, axis=0)


def _retrieve_kernel(qp_ref, keys_ref, flat_ref, ii_ref, jj_ref, gg_ref, *, dhalf, nkeys, topk):
    tl = qp_ref.shape[0]
    qb = qp_ref[...].astype(BF16)
    key_ids = lax.broadcasted_iota(I32, (nkeys, tl), 0)
    sv, si = [], []
    for half in range(2):
        s = lax.dot_general(keys_ref[half], qb[:, half * dhalf:(half + 1) * dhalf],
                            (((1,), (1,)), ((), ())), preferred_element_type=F32)
        v, i = _extract_topk(s, key_ids, topk)
        sv.append(v)
        si.append(i)

    pairs = _staircase(topk)
    npad = flat_ref.shape[0]
    cand = [sv[0][a:a + 1, :] + sv[1][b:b + 1, :] for a, b in pairs]
    cand += [jnp.full((1, tl), NEG_INF, F32)] * (npad - len(pairs))
    cand = jnp.concatenate(cand, axis=0)
    cv, ci = _extract_topk(cand, flat_ref[...], topk)

    rank = lax.broadcasted_iota(I32, (topk, tl), 0)
    ia = lax.shift_right_logical(ci, int(math.log2(topk)))
    ib = ci & (topk - 1)
    ii_rows, jj_rows = [], []
    for r in range(topk):
        ii_rows.append(jnp.sum(jnp.where(rank == ia[r:r + 1, :], si[0], 0), axis=0, keepdims=True))
        jj_rows.append(jnp.sum(jnp.where(rank == ib[r:r + 1, :], si[1], 0), axis=0, keepdims=True))
    ii_ref[...] = jnp.concatenate(ii_rows, axis=0)
    jj_ref[...] = jnp.concatenate(jj_rows, axis=0)
    e = jnp.exp(cv - cv[0:1, :])
    gg_ref[...] = e / jnp.sum(e, axis=0, keepdims=True)


def _retrieve(qp, keys, *, tl):
    t, nq = qp.shape
    _, nkeys, dhalf = keys.shape
    heads = nq // (2 * dhalf)
    assert 2 * dhalf == LANES and PEER_TOPK & (PEER_TOPK - 1) == 0
    pairs = _staircase(PEER_TOPK)
    npad = -(-len(pairs) // 8) * 8
    flat = np.full((npad, tl), np.iinfo(np.int32).max - 1, np.int32)
    flat[:len(pairs), :] = np.asarray([a * PEER_TOPK + b for a, b in pairs], np.int32)[:, None]
    kern = functools.partial(_retrieve_kernel, dhalf=dhalf, nkeys=nkeys, topk=PEER_TOPK)
    out = jax.ShapeDtypeStruct((heads * PEER_TOPK, t), I32)
    blk = pl.BlockSpec((PEER_TOPK, tl), lambda i, h: (h, i))
    return pl.pallas_call(
        kern,
        grid=(t // tl, heads),
        in_specs=[pl.BlockSpec((tl, LANES), lambda i, h: (i, h)), _resident(keys.shape),
                  _resident(flat.shape)],
        out_specs=[blk, blk, blk],
        out_shape=[out, out, jax.ShapeDtypeStruct((heads * PEER_TOPK, t), F32)],
        compiler_params=_params("arbitrary", "arbitrary"),
        name="retrieve",
    )(qp, keys, jnp.asarray(flat))


def _gelu_tanh(a):
    return 0.5 * a * (1.0 + jnp.tanh(math.sqrt(2.0 / math.pi) * (a + 0.044715 * a * a * a)))


def _peer_act_kernel(x_ref, u_ref, ii_ref, jj_ref, a_ref, *, groups):
    c = pl.program_id(1)

    @pl.when(c == 0)
    def _():
        a_ref[...] = jnp.zeros(a_ref.shape, F32)

    a_all = lax.dot_general(x_ref[...], u_ref[...], (((1,), (1,)), ((), ())),
                            preferred_element_type=F32)
    ii = ii_ref[...]
    jj = jj_ref[...]
    acc = a_ref[...]
    for g in range(groups):
        picked = jnp.take_along_axis(a_all[:, g * LANES:(g + 1) * LANES], jj, axis=1)
        acc = jnp.where(ii == c * groups + g, picked, acc)
    a_ref[...] = acc


def _peer_act(u2, utab, ii, jj, *, tb, groups):
    t, d = u2.shape
    ne = utab.shape[0]
    ce = groups * LANES
    tok = lambda i, c: (i, 0)
    return pl.pallas_call(
        functools.partial(_peer_act_kernel, groups=groups),
        grid=(t // tb, ne // ce),
        in_specs=[pl.BlockSpec((tb, d), tok), pl.BlockSpec((ce, d), lambda i, c: (c, 0)),
                  pl.BlockSpec((tb, LANES), tok), pl.BlockSpec((tb, LANES), tok)],
        out_specs=pl.BlockSpec((tb, LANES), tok),
        out_shape=jax.ShapeDtypeStruct((t, LANES), F32),
        compiler_params=_params("arbitrary", "arbitrary"),
        name="peer_act",
    )(u2, utab, ii, jj)


SCATTER_UNROLL = 16


def _w_pitch(tb):
    p = -(-tb // 8)
    return 8 * (p if p % 2 else p + 1)


def _peer_out_kernel(a_ref, gg_ref, ii_ref, jj_ref, v_ref, x1_ref, g2_ref, lg_ref, lb_ref,
                     o_ref, w_sc, acc_sc, *, tb, groups, nchunks, alpha):
    c = pl.program_id(1)
    pitch = _w_pitch(tb)

    @pl.when(c == 0)
    def _():
        acc_sc[...] = jnp.zeros(acc_sc.shape, F32)
        sub = lax.broadcasted_iota(I32, (LANES, LANES), 0)

        def scatter(tok, carry):
            wrow = _gelu_tanh(a_ref[pl.ds(tok, 1), :]) * gg_ref[pl.ds(tok, 1), :]
            lhs = jnp.where(sub == ii_ref[pl.ds(tok, 1), :], wrow, 0.0).astype(BF16)
            rhs = jnp.where(sub == jj_ref[pl.ds(tok, 1), :], 1.0, 0.0).astype(BF16)
            w_sc[pl.ds(tok, LANES, stride=pitch), :] = lax.dot_general(
                lhs, rhs, (((1,), (1,)), ((), ())), preferred_element_type=F32)
            return carry

        lax.fori_loop(0, tb, scatter, 0, unroll=SCATTER_UNROLL)

    parts = [w_sc[pl.ds(pl.multiple_of((c * groups + g) * pitch, 8), tb), :].astype(BF16)
             for g in range(groups)]
    acc_sc[...] += jnp.dot(jnp.concatenate(parts, axis=1), v_ref[...],
                           preferred_element_type=F32)

    @pl.when(c == nchunks - 1)
    def _():
        r = alpha * x1_ref[...] + (1.0 + g2_ref[...]) * acc_sc[...]
        o_ref[...] = _normalize(r) * lg_ref[...] + lb_ref[...]


def _peer_out(a, gg, ii, jj, vtab, x1, g2, lg, lb, *, seq, alpha, tb, groups):
    t, d = x1.shape
    ne = vtab.shape[0]
    assert ne == LANES * LANES
    ce = groups * LANES
    nchunks = ne // ce
    tpb = seq // tb
    tok = lambda i, c: (i, 0)
    kern = functools.partial(_peer_out_kernel, tb=tb, groups=groups, nchunks=nchunks, alpha=alpha)
    return pl.pallas_call(
        kern,
        grid=(t // tb, nchunks),
        in_specs=[pl.BlockSpec((tb, LANES), tok), pl.BlockSpec((tb, LANES), tok),
                  pl.BlockSpec((tb, LANES), tok), pl.BlockSpec((tb, LANES), tok),
                  pl.BlockSpec((ce, d), lambda i, c: (c, 0)),
                  pl.BlockSpec((tb, d), tok, pipeline_mode=pl.Buffered(1)),
                  pl.BlockSpec((None, 1, d), lambda i, c: (i // tpb, 0, 0)),
                  _resident((1, d)), _resident((1, d))],
        out_specs=pl.BlockSpec((tb, d), tok),
        out_shape=jax.ShapeDtypeStruct((t, d), F32),
        scratch_shapes=[pltpu.VMEM((LANES * _w_pitch(tb), LANES), F32), pltpu.VMEM((tb, d), F32)],
        compiler_params=_params("arbitrary", "arbitrary"),
        name="peer_out",
    )(a, gg, ii, jj, vtab, x1, g2, lg.reshape(1, d), lb.reshape(1, d))


def _tile(n, want):
    t = min(n, want)
    while n % t:
        t -= 1
    return t


def kernel(x, c, w_mod, b_mod, w_in, b_f, conv_w, conv_b, conv_ln_g, conv_ln_b,
           w_attn_out, w_conv_out, w_out, ln1_g, ln1_b, peer_wq, peer_sub_keys,
           peer_u, peer_v, ln2_g, ln2_b):
    bsz, seq, d = x.shape
    depth = w_mod.shape[0]
    t = bsz * seq
    att_w = w_attn_out.shape[1]
    n_heads = b_f.shape[1]
    head_dim = att_w // n_heads
    conv_c = conv_w.shape[2]
    alpha = (2.0 * depth) ** 0.25
    assert n_heads <= LANES and LANES % head_dim == 0 and att_w % LANES == 0

    tm = _tile(seq, 512)
    xf = x.reshape(t, d)
    for l in range(depth):
        mod = _mod(c, w_mod[l], b_mod[l])
        sh1, sc1, g1, sh2, sc2, g2 = [m.reshape(bsz, 1, d) for m in jnp.split(mod, 6, axis=-1)]

        o_f = 3 * att_w
        o_c = o_f + n_heads
        o_g = o_c + 2 * conv_c
        wl = w_in[l]
        wqkv = wl[:, :o_f].astype(BF16)
        wf = jnp.pad(wl[:, o_f:o_c], ((0, 0), (0, LANES - n_heads))).astype(BF16)
        wc = wl[:, o_c:o_g].astype(BF16)
        wg = wl[:, o_g:].astype(BF16)
        qkv, f, h, gates = _inproj(xf, sh1, sc1, wqkv, wf, wc, wg, seq=seq, att_w=att_w,
                                   conv_c=conv_c, head_dim=head_dim, tm=tm)

        bf_row = jnp.pad(b_f[l], (0, LANES - n_heads)).reshape(1, LANES)
        qb, kb = _fcum(f.reshape(bsz, seq, LANES), bf_row, n_heads=n_heads, tc=_tile(seq, 256))
        attn = _attention(qkv, qb.reshape(t, -1), kb.reshape(t, -1), bsz=bsz, seq=seq,
                          att_w=att_w, head_dim=head_dim, tq=tm, tk=tm)

        hc = _conv(h, conv_w[l], conv_b[l], conv_ln_g[l], conv_ln_b[l], seq=seq, ts=tm)

        x1, u2, qp = _merge(attn, hc, gates, xf, g1, sh2, sc2,
                            w_attn_out[l].astype(BF16), w_conv_out[l].astype(BF16),
                            w_out[l].astype(BF16), ln1_g[l], ln1_b[l],
                            peer_wq[l].astype(BF16), seq=seq, alpha=alpha, tm=tm)

        ii_t, jj_t, gg_t = _retrieve(qp, peer_sub_keys[l].astype(BF16), tl=tm)
        ii, jj, gg = ii_t.T, jj_t.T, gg_t.T
        a = _peer_act(u2, peer_u[l].astype(BF16), ii, jj, tb=tm, groups=16)
        xf = _peer_out(a, gg, ii, jj, peer_v[l].astype(BF16), x1, g2, ln2_g[l], ln2_b[l],
                       seq=seq, alpha=alpha, tb=tm, groups=8)
    return xf.reshape(bsz, seq, d)
```

```python
import functools
import math

import jax
import jax.numpy as jnp
import numpy as np
from jax import lax
from jax.experimental import pallas as pl
from jax.experimental.pallas import tpu as pltpu

F32 = jnp.float32
BF16 = jnp.bfloat16
I32 = jnp.int32

LN_EPS = 1e-5
PEER_TOPK = 16
LANES = 128
VMEM_LIMIT_BYTES = 56 * 1024 * 1024
NEG_INF = float("-inf")
LOG2_E = math.log2(math.e)


def _params(*semantics):
    return pltpu.CompilerParams(dimension_semantics=semantics, vmem_limit_bytes=VMEM_LIMIT_BYTES)


def _normalize(x):
    mu = jnp.mean(x, axis=-1, keepdims=True)
    xc = x - mu
    var = jnp.mean(xc * xc, axis=-1, keepdims=True)
    return xc * lax.rsqrt(var + LN_EPS)


def _resident(shape):
    nd = len(shape)
    return pl.BlockSpec(shape, lambda *_: (0,) * nd)


def _mod_kernel(c_ref, w_ref, b_ref, o_ref):
    c = c_ref[...]
    cs = c * jax.nn.sigmoid(c)
    o_ref[...] = jnp.dot(cs, w_ref[...], preferred_element_type=F32,
                         precision=lax.Precision.HIGHEST) + b_ref[...]


def _mod(c, w, b):
    bsz, d = c.shape
    n = w.shape[1]
    rows = -(-bsz // 8) * 8
    tn = math.gcd(n, 1536)
    cp = jnp.pad(c, ((0, rows - bsz), (0, 0)))
    out = pl.pallas_call(
        _mod_kernel,
        grid=(n // tn,),
        in_specs=[pl.BlockSpec((rows, d), lambda j: (0, 0)),
                  pl.BlockSpec((d, tn), lambda j: (0, j)),
                  pl.BlockSpec((1, tn), lambda j: (0, j))],
        out_specs=pl.BlockSpec((rows, tn), lambda j: (0, j)),
        out_shape=jax.ShapeDtypeStruct((rows, n), F32),
        compiler_params=_params("arbitrary"),
        name="mod",
    )(cp, w, b.reshape(1, n))
    return out[:bsz]


def _inproj_kernel(x_ref, sh_ref, sc_ref, wqkv_ref, wf_ref, wc_ref, wg_ref,
                   qkv_ref, f_ref, h_ref, g_ref, *, att_w, conv_c, qscale):
    y = _normalize(x_ref[...])
    u = (y * (1.0 + sc_ref[...]) + sh_ref[...]).astype(BF16)
    qkv = jnp.dot(u, wqkv_ref[...], preferred_element_type=F32)
    qkv_ref[:, :att_w] = (qkv[:, :att_w] * qscale).astype(BF16)
    qkv_ref[:, att_w:] = qkv[:, att_w:].astype(BF16)
    f_ref[...] = jnp.dot(u, wf_ref[...], preferred_element_type=F32)
    cab = jnp.dot(u, wc_ref[...], preferred_element_type=F32)
    h_ref[...] = cab[:, :conv_c] * jax.nn.sigmoid(cab[:, conv_c:])
    g_ref[...] = jax.nn.sigmoid(jnp.dot(u, wg_ref[...], preferred_element_type=F32))


def _inproj(x2, sh, sc, wqkv, wf, wc, wg, *, seq, att_w, conv_c, head_dim, tm):
    t, d = x2.shape
    tpb = seq // tm
    row = lambda i: (i, 0)
    per_batch = pl.BlockSpec((None, 1, d), lambda i: (i // tpb, 0, 0))
    kern = functools.partial(_inproj_kernel, att_w=att_w, conv_c=conv_c,
                             qscale=LOG2_E / math.sqrt(head_dim))
    return pl.pallas_call(
        kern,
        grid=(t // tm,),
        in_specs=[pl.BlockSpec((tm, d), row), per_batch, per_batch,
                  _resident(wqkv.shape), _resident(wf.shape), _resident(wc.shape),
                  _resident(wg.shape)],
        out_specs=[pl.BlockSpec((tm, 3 * att_w), row), pl.BlockSpec((tm, LANES), row),
                   pl.BlockSpec((tm, conv_c), row), pl.BlockSpec((tm, 2 * d), row)],
        out_shape=[jax.ShapeDtypeStruct((t, 3 * att_w), BF16),
                   jax.ShapeDtypeStruct((t, LANES), F32),
                   jax.ShapeDtypeStruct((t, conv_c), F32),
                   jax.ShapeDtypeStruct((t, 2 * d), F32)],
        compiler_params=_params("arbitrary"),
        name="inproj",
    )(x2, sh, sc, wqkv, wf, wc, wg)


BIAS_TERMS = 3


def _fcum_kernel(f_ref, bf_ref, qb_ref, kb_ref, carry_ref, *, tc, n_heads):
    @pl.when(pl.program_id(1) == 0)
    def _():
        carry_ref[...] = jnp.zeros(carry_ref.shape, F32)

    z = f_ref[...] + bf_ref[...]
    logf = jnp.minimum(z, 0.0) - jnp.log1p(jnp.exp(-jnp.abs(z)))
    r = lax.broadcasted_iota(I32, (tc, tc), 0)
    c = lax.broadcasted_iota(I32, (tc, tc), 1)
    tri = jnp.where(c <= r, 1.0, 0.0).astype(F32)
    cs = jnp.dot(tri, logf, preferred_element_type=F32,
                 precision=lax.Precision.HIGHEST) + carry_ref[...]
    carry_ref[...] = cs[tc - 1:tc, :]
    rem = cs * LOG2_E
    terms = []
    for _ in range(BIAS_TERMS):
        term = rem.astype(BF16)
        terms.append(term.astype(F32))
        rem = rem - terms[-1]
    lane = lax.broadcasted_iota(I32, (tc, LANES), 1)
    for hh in range(n_heads):
        qb = jnp.where((lane >= BIAS_TERMS) & (lane < 2 * BIAS_TERMS), -1.0, 0.0)
        kb = jnp.where(lane < BIAS_TERMS, 1.0, 0.0)
        for r, term in enumerate(terms):
            col = jnp.broadcast_to(term[:, hh:hh + 1], (tc, LANES))
            qb = jnp.where(lane == r, col, qb)
            kb = jnp.where(lane == BIAS_TERMS + r, col, kb)
        qb_ref[:, hh * LANES:(hh + 1) * LANES] = qb.astype(BF16)
        kb_ref[:, hh * LANES:(hh + 1) * LANES] = kb.astype(BF16)


def _fcum(f3, bf_row, *, n_heads, tc):
    bsz, seq, w = f3.shape
    slab = pl.BlockSpec((None, tc, n_heads * LANES), lambda b, j: (b, j, 0))
    out = jax.ShapeDtypeStruct((bsz, seq, n_heads * LANES), BF16)
    return pl.pallas_call(
        functools.partial(_fcum_kernel, tc=tc, n_heads=n_heads),
        grid=(bsz, seq // tc),
        in_specs=[pl.BlockSpec((None, tc, w), lambda b, j: (b, j, 0)),
                  pl.BlockSpec((1, w), lambda b, j: (0, 0))],
        out_specs=[slab, slab],
        out_shape=[out, out],
        scratch_shapes=[pltpu.VMEM((1, w), F32)],
        compiler_params=_params("arbitrary", "arbitrary"),
        name="fcum",
    )(f3, bf_row)


ATTN_MASKED = 1
ATTN_LAST = 2


def _attn_tables(nq, tq, tk):
    qi_l, ki_l, fl_l = [], [], []
    for qi in range(nq):
        last = (qi * tq + tq - 1) // tk
        for ki in range(last + 1):
            masked = ki * tk + tk - 1 > qi * tq
            qi_l.append(qi)
            ki_l.append(ki)
            fl_l.append((ATTN_MASKED if masked else 0) | (ATTN_LAST if ki == last else 0))
    as_i32 = lambda v: jnp.asarray(np.asarray(v, np.int32))
    return as_i32(qi_l), as_i32(ki_l), as_i32(fl_l)


def _attn_kernel(qi_tab, ki_tab, fl_tab, q_ref, qb_ref, k_ref, kb_ref, v_ref, o_ref,
                 qop_sc, m_sc, l_sc, acc_sc, *, tq, tk, dh, heads):
    p = pl.program_id(2)
    qi = qi_tab[p]
    ki = ki_tab[p]
    fl = fl_tab[p]
    head_of_lane = lax.broadcasted_iota(I32, (tq, LANES), 1) // dh

    @pl.when(ki == 0)
    def _():
        m_sc[...] = jnp.full(m_sc.shape, NEG_INF, F32)
        l_sc[...] = jnp.zeros(l_sc.shape, F32)
        acc_sc[...] = jnp.zeros(acc_sc.shape, F32)
        q = q_ref[...]
        for hh in range(heads):
            qop_sc[hh, :, :LANES] = jnp.where(head_of_lane == hh, q, jnp.zeros_like(q))
            qop_sc[hh, :, LANES:] = qb_ref[:, hh * LANES:(hh + 1) * LANES]

    def per_head(vals):
        out = vals[0]
        for hh in range(1, heads):
            out = jnp.where(head_of_lane == hh, vals[hh], out)
        return out

    def step(masked):
        if masked:
            row = qi * tq + lax.broadcasted_iota(I32, (tq, tk), 0)
            col = ki * tk + lax.broadcasted_iota(I32, (tq, tk), 1)
            keep = col <= row
        alphas, pvs = [], []
        k = k_ref[...]
        for hh in range(heads):
            kop = jnp.concatenate([k, kb_ref[:, hh * LANES:(hh + 1) * LANES]], axis=1)
            s = lax.dot_general(qop_sc[hh], kop, (((1,), (1,)), ((), ())),
                                preferred_element_type=F32)
            if masked:
                s = jnp.where(keep, s, NEG_INF)
            m_prev = m_sc[hh]
            m_new = jnp.maximum(m_prev, jnp.max(s, axis=-1, keepdims=True))
            alpha = jnp.exp2(m_prev - m_new)
            pr = jnp.exp2(s - jnp.concatenate([m_new] * (tk // LANES), axis=1))
            l_sc[hh] = alpha * l_sc[hh] + jnp.sum(pr, axis=-1, keepdims=True)
            m_sc[hh] = m_new
            alphas.append(alpha)
            pvs.append(jnp.dot(pr.astype(BF16), v_ref[...], preferred_element_type=F32))
        acc_sc[...] = per_head(alphas) * acc_sc[...] + per_head(pvs)

    pl.when((fl & ATTN_MASKED) != 0)(lambda: step(True))
    pl.when((fl & ATTN_MASKED) == 0)(lambda: step(False))

    @pl.when((fl & ATTN_LAST) != 0)
    def _():
        o_ref[...] = (acc_sc[...] / per_head([l_sc[hh] for hh in range(heads)])).astype(o_ref.dtype)


def _attention(qkv, qb, kb, *, bsz, seq, att_w, head_dim, tq, tk):
    heads = LANES // head_dim
    ncol = att_w // LANES
    nq, nk = seq // tq, seq // tk
    assert tk % LANES == 0
    qi_tab, ki_tab, fl_tab = _attn_tables(nq, tq, tk)
    kern = functools.partial(_attn_kernel, tq=tq, tk=tk, dh=head_dim, heads=heads)
    qrow = lambda b, c, p, qt, kt, ft: b * nq + qt[p]
    krow = lambda b, c, p, qt, kt, ft: b * nk + kt[p]
    grid_spec = pltpu.PrefetchScalarGridSpec(
        num_scalar_prefetch=3,
        grid=(bsz, ncol, int(qi_tab.shape[0])),
        in_specs=[
            pl.BlockSpec((tq, LANES), lambda b, c, *a: (qrow(b, c, *a), c)),
            pl.BlockSpec((tq, heads * LANES), lambda b, c, *a: (qrow(b, c, *a), c)),
            pl.BlockSpec((tk, LANES), lambda b, c, *a: (krow(b, c, *a), ncol + c)),
            pl.BlockSpec((tk, heads * LANES), lambda b, c, *a: (krow(b, c, *a), c)),
            pl.BlockSpec((tk, LANES), lambda b, c, *a: (krow(b, c, *a), 2 * ncol + c)),
        ],
        out_specs=pl.BlockSpec((tq, LANES), lambda b, c, *a: (qrow(b, c, *a), c)),
        scratch_shapes=[pltpu.VMEM((heads, tq, 2 * LANES), BF16),
                        pltpu.VMEM((heads, tq, LANES), F32), pltpu.VMEM((heads, tq, LANES), F32),
                        pltpu.VMEM((tq, LANES), F32)],
    )
    return pl.pallas_call(
        kern,
        grid_spec=grid_spec,
        out_shape=jax.ShapeDtypeStruct((bsz * seq, att_w), BF16),
        compiler_params=_params("arbitrary", "arbitrary", "arbitrary"),
        name="attn",
    )(qi_tab, ki_tab, fl_tab, qkv, qb, qkv, kb, qkv)


CONV_HALO = 32
CONV_CHUNK = 64


SUBLANES = 8


def _conv_kernel(h_ref, halo_ref, w_ref, b_ref, g_ref, beta_ref, o_ref, sh_sc, *, ts, taps, tpb):
    first = pl.program_id(0) % tpb == 0
    rows = ts + CONV_HALO - SUBLANES
    sh_sc[0, 0:CONV_HALO, :] = jnp.where(first, 0.0, halo_ref[...])
    sh_sc[0, CONV_HALO:, :] = h_ref[...]
    for r in range(1, SUBLANES):
        sh_sc[r, 0:rows, :] = sh_sc[0, r:r + rows, :]
    lead = CONV_HALO - (taps - 1)
    for c0 in range(0, ts, CONV_CHUNK):
        acc = jnp.broadcast_to(b_ref[...], (CONV_CHUNK, b_ref.shape[1]))
        for k in range(taps):
            r = (lead + k) % SUBLANES
            start = c0 + lead + k - r
            acc = acc + w_ref[k:k + 1, :] * sh_sc[r, start:start + CONV_CHUNK, :]
        y = _normalize(acc) * g_ref[...] + beta_ref[...]
        o_ref[c0:c0 + CONV_CHUNK, :] = (y * jax.nn.sigmoid(y)).astype(o_ref.dtype)


def _conv(h, w, b, g, beta, *, seq, ts):
    t, c = h.shape
    taps = w.shape[0]
    assert taps - 1 <= CONV_HALO and ts % CONV_CHUNK == 0 and ts % CONV_HALO == 0
    tpb = seq // ts
    hb = ts // CONV_HALO
    vec = lambda v: v.reshape(1, c)
    return pl.pallas_call(
        functools.partial(_conv_kernel, ts=ts, taps=taps, tpb=tpb),
        grid=(t // ts,),
        in_specs=[pl.BlockSpec((ts, c), lambda i: (i, 0)),
                  pl.BlockSpec((CONV_HALO, c), lambda i: (jnp.maximum(i * hb - 1, 0), 0)),
                  _resident((taps, c)), _resident((1, c)), _resident((1, c)), _resident((1, c))],
        out_specs=pl.BlockSpec((ts, c), lambda i: (i, 0)),
        out_shape=jax.ShapeDtypeStruct((t, c), BF16),
        scratch_shapes=[pltpu.VMEM((SUBLANES, ts + CONV_HALO, c), F32)],
        compiler_params=_params("arbitrary"),
        name="conv",
    )(h, h, w, vec(b), vec(g), vec(beta))


def _merge_kernel(attn_ref, hc_ref, g_ref, x_ref, g1_ref, sh2_ref, sc2_ref, wa_ref, wc_ref,
                  wo_ref, lg_ref, lb_ref, wq_ref, x1_ref, u2_ref, qp_ref, *, alpha):
    d = x_ref.shape[1]
    ya = jnp.dot(attn_ref[...], wa_ref[...], preferred_element_type=F32)
    yb = jnp.dot(hc_ref[...], wc_ref[...], preferred_element_type=F32)
    merged = (g_ref[:, :d] * ya + g_ref[:, d:] * yb).astype(BF16)
    out = jnp.dot(merged, wo_ref[...], preferred_element_type=F32)
    x1 = _normalize(alpha * x_ref[...] + (1.0 + g1_ref[...]) * out) * lg_ref[...] + lb_ref[...]
    x1_ref[...] = x1
    u2 = (_normalize(x1) * (1.0 + sc2_ref[...]) + sh2_ref[...]).astype(BF16)
    u2_ref[...] = u2
    qp_ref[...] = jnp.dot(u2, wq_ref[...], preferred_element_type=F32)


def _merge(attn, hc, gates, x2, g1, sh2, sc2, wa, wc, wo, lg, lb, wq, *, seq, alpha, tm):
    t, d = x2.shape
    tpb = seq // tm
    row = lambda i: (i, 0)
    per_batch = pl.BlockSpec((None, 1, d), lambda i: (i // tpb, 0, 0))
    nq = wq.shape[1]
    return pl.pallas_call(
        functools.partial(_merge_kernel, alpha=alpha),
        grid=(t // tm,),
        in_specs=[pl.BlockSpec((tm, attn.shape[1]), row), pl.BlockSpec((tm, hc.shape[1]), row),
                  pl.BlockSpec((tm, 2 * d), row), pl.BlockSpec((tm, d), row),
                  per_batch, per_batch, per_batch,
                  _resident(wa.shape), _resident(wc.shape), _resident(wo.shape),
                  _resident((1, d)), _resident((1, d)), _resident(wq.shape)],
        out_specs=[pl.BlockSpec((tm, d), row), pl.BlockSpec((tm, d), row),
                   pl.BlockSpec((tm, nq), row)],
        out_shape=[jax.ShapeDtypeStruct((t, d), F32), jax.ShapeDtypeStruct((t, d), BF16),
                   jax.ShapeDtypeStruct((t, nq), F32)],
        compiler_params=_params("arbitrary"),
        name="merge",
    )(attn, hc, gates, x2, g1, sh2, sc2, wa, wc, wo, lg.reshape(1, d), lb.reshape(1, d), wq)


def _staircase(k):
    return [(a, b) for a in range(k) for b in range(k) if (a + 1) * (b + 1) <= k]


ID_PAD = 1e9


def _extract_topk(s, ids, k):
    vals, idxs = [], []
    for _ in range(k):
        m = jnp.max(s, axis=0, keepdims=True)
        i = jnp.min(jnp.where(s == m, ids, ID_PAD), axis=0, keepdims=True)
        s = jnp.where(ids == i, NEG_INF, s)
        vals.append(m)
        idxs.append(i)
    return jnp.concatenate(vals, axis=0), jnp.concatenate(idxs, axis=0)


RETRIEVE_HEADS = 2


def _retrieve_kernel(qp_ref, keys_ref, flat_ref, ii_ref, jj_ref, gg_ref, *, dhalf, nkeys, topk):
    tl = qp_ref.shape[0]
    qb = qp_ref[...].astype(BF16)
    key_ids = lax.broadcasted_iota(I32, (nkeys, tl), 0).astype(F32)
    pairs = _staircase(topk)
    npad = flat_ref.shape[0]
    rank = lax.broadcasted_iota(I32, (topk, tl), 0)
    for hh in range(RETRIEVE_HEADS):
        sv, si = [], []
        for half in range(2):
            col = (2 * hh + half) * dhalf
            s = lax.dot_general(keys_ref[half], qb[:, col:col + dhalf],
                                (((1,), (1,)), ((), ())), preferred_element_type=F32)
            v, i = _extract_topk(s, key_ids, topk)
            sv.append(v)
            si.append(i)

        cand = [sv[0][a:a + 1, :] + sv[1][b:b + 1, :] for a, b in pairs]
        cand += [jnp.full((1, tl), NEG_INF, F32)] * (npad - len(pairs))
        cand = jnp.concatenate(cand, axis=0)
        cv, ci = _extract_topk(cand, flat_ref[...], topk)

        ci = ci.astype(I32)
        ia = lax.shift_right_logical(ci, int(math.log2(topk)))
        ib = ci & (topk - 1)
        ii_rows, jj_rows = [], []
        for r in range(topk):
            ii_rows.append(jnp.sum(jnp.where(rank == ia[r:r + 1, :], si[0], 0.0), axis=0,
                                   keepdims=True))
            jj_rows.append(jnp.sum(jnp.where(rank == ib[r:r + 1, :], si[1], 0.0), axis=0,
                                   keepdims=True))
        rows = slice(hh * topk, (hh + 1) * topk)
        ii_ref[rows, :] = jnp.concatenate(ii_rows, axis=0).astype(I32)
        jj_ref[rows, :] = jnp.concatenate(jj_rows, axis=0).astype(I32)
        e = jnp.exp(cv - cv[0:1, :])
        gg_ref[rows, :] = e / jnp.sum(e, axis=0, keepdims=True)


def _retrieve(qp, keys, *, tl):
    t, nq = qp.shape
    _, nkeys, dhalf = keys.shape
    heads = nq // (2 * dhalf)
    assert 2 * dhalf == LANES and PEER_TOPK & (PEER_TOPK - 1) == 0 and heads % RETRIEVE_HEADS == 0
    pairs = _staircase(PEER_TOPK)
    npad = -(-len(pairs) // 8) * 8
    flat = np.full((npad, tl), ID_PAD / 2, np.float32)
    flat[:len(pairs), :] = np.asarray([a * PEER_TOPK + b for a, b in pairs], np.float32)[:, None]
    kern = functools.partial(_retrieve_kernel, dhalf=dhalf, nkeys=nkeys, topk=PEER_TOPK)
    out = jax.ShapeDtypeStruct((heads * PEER_TOPK, t), I32)
    blk = pl.BlockSpec((RETRIEVE_HEADS * PEER_TOPK, tl), lambda i, h: (h, i))
    return pl.pallas_call(
        kern,
        grid=(t // tl, heads // RETRIEVE_HEADS),
        in_specs=[pl.BlockSpec((tl, RETRIEVE_HEADS * LANES), lambda i, h: (i, h)),
                  _resident(keys.shape),
                  _resident(flat.shape)],
        out_specs=[blk, blk, blk],
        out_shape=[out, out, jax.ShapeDtypeStruct((heads * PEER_TOPK, t), F32)],
        compiler_params=_params("arbitrary", "arbitrary"),
        name="retrieve",
    )(qp, keys, jnp.asarray(flat))


def _gelu_tanh(a):
    return 0.5 * a * (1.0 + jnp.tanh(math.sqrt(2.0 / math.pi) * (a + 0.044715 * a * a * a)))


def _peer_act_kernel(x_ref, u_ref, ii_ref, jj_ref, a_ref, *, groups):
    c = pl.program_id(1)

    @pl.when(c == 0)
    def _():
        a_ref[...] = jnp.zeros(a_ref.shape, F32)

    a_all = lax.dot_general(x_ref[...], u_ref[...], (((1,), (1,)), ((), ())),
                            preferred_element_type=F32)
    ii = ii_ref[...]
    jj = jj_ref[...]
    acc = a_ref[...]
    for g in range(groups):
        picked = jnp.take_along_axis(a_all[:, g * LANES:(g + 1) * LANES], jj, axis=1)
        acc = jnp.where(ii == c * groups + g, picked, acc)
    a_ref[...] = acc


def _peer_act(u2, utab, ii, jj, *, tb, groups):
    t, d = u2.shape
    ne = utab.shape[0]
    ce = groups * LANES
    tok = lambda i, c: (i, 0)
    return pl.pallas_call(
        functools.partial(_peer_act_kernel, groups=groups),
        grid=(t // tb, ne // ce),
        in_specs=[pl.BlockSpec((tb, d), tok), pl.BlockSpec((ce, d), lambda i, c: (c, 0)),
                  pl.BlockSpec((tb, LANES), tok), pl.BlockSpec((tb, LANES), tok)],
        out_specs=pl.BlockSpec((tb, LANES), tok),
        out_shape=jax.ShapeDtypeStruct((t, LANES), F32),
        compiler_params=_params("arbitrary", "arbitrary"),
        name="peer_act",
    )(u2, utab, ii, jj)


SCATTER_UNROLL = 32


def _w_pitch(tb):
    p = -(-tb // 8)
    return 8 * (p if p % 2 else p + 1)


def _peer_out_kernel(a_ref, gg_ref, ii_ref, jj_ref, v_ref, x1_ref, g2_ref, lg_ref, lb_ref,
                     o_ref, w_sc, acc_sc, *, tb, groups, nchunks, alpha):
    c = pl.program_id(1)
    pitch = _w_pitch(tb)

    @pl.when(c == 0)
    def _():
        acc_sc[...] = jnp.zeros(acc_sc.shape, F32)
        sub = lax.broadcasted_iota(I32, (LANES, LANES), 0)

        def scatter(tok, carry):
            wrow = _gelu_tanh(a_ref[pl.ds(tok, 1), :]) * gg_ref[pl.ds(tok, 1), :]
            lhs = jnp.where(sub == ii_ref[pl.ds(tok, 1), :], wrow, 0.0).astype(BF16)
            rhs = jnp.where(sub == jj_ref[pl.ds(tok, 1), :], 1.0, 0.0).astype(BF16)
            w_sc[pl.ds(tok, LANES, stride=pitch), :] = lax.dot_general(
                lhs, rhs, (((1,), (1,)), ((), ())), preferred_element_type=F32)
            return carry

        lax.fori_loop(0, tb, scatter, 0, unroll=SCATTER_UNROLL)

    parts = [w_sc[pl.ds(pl.multiple_of((c * groups + g) * pitch, 8), tb), :].astype(BF16)
             for g in range(groups)]
    acc_sc[...] += jnp.dot(jnp.concatenate(parts, axis=1), v_ref[...],
                           preferred_element_type=F32)

    @pl.when(c == nchunks - 1)
    def _():
        r = alpha * x1_ref[...] + (1.0 + g2_ref[...]) * acc_sc[...]
        o_ref[...] = _normalize(r) * lg_ref[...] + lb_ref[...]


def _peer_out(a, gg, ii, jj, vtab, x1, g2, lg, lb, *, seq, alpha, tb, groups):
    t, d = x1.shape
    ne = vtab.shape[0]
    assert ne == LANES * LANES
    ce = groups * LANES
    nchunks = ne // ce
    tpb = seq // tb
    tok = lambda i, c: (i, 0)
    kern = functools.partial(_peer_out_kernel, tb=tb, groups=groups, nchunks=nchunks, alpha=alpha)
    return pl.pallas_call(
        kern,
        grid=(t // tb, nchunks),
        in_specs=[pl.BlockSpec((tb, LANES), tok), pl.BlockSpec((tb, LANES), tok),
                  pl.BlockSpec((tb, LANES), tok), pl.BlockSpec((tb, LANES), tok),
                  pl.BlockSpec((ce, d), lambda i, c: (c, 0)),
                  pl.BlockSpec((tb, d), tok, pipeline_mode=pl.Buffered(1)),
                  pl.BlockSpec((None, 1, d), lambda i, c: (i // tpb, 0, 0)),
                  _resident((1, d)), _resident((1, d))],
        out_specs=pl.BlockSpec((tb, d), tok),
        out_shape=jax.ShapeDtypeStruct((t, d), F32),
        scratch_shapes=[pltpu.VMEM((LANES * _w_pitch(tb), LANES), F32), pltpu.VMEM((tb, d), F32)],
        compiler_params=_params("arbitrary", "arbitrary"),
        name="peer_out",
    )(a, gg, ii, jj, vtab, x1, g2, lg.reshape(1, d), lb.reshape(1, d))


def _tile(n, want):
    t = min(n, want)
    while n % t:
        t -= 1
    return t


def kernel(x, c, w_mod, b_mod, w_in, b_f, conv_w, conv_b, conv_ln_g, conv_ln_b,
           w_attn_out, w_conv_out, w_out, ln1_g, ln1_b, peer_wq, peer_sub_keys,
           peer_u, peer_v, ln2_g, ln2_b):
    bsz, seq, d = x.shape
    depth = w_mod.shape[0]
    t = bsz * seq
    att_w = w_attn_out.shape[1]
    n_heads = b_f.shape[1]
    head_dim = att_w // n_heads
    conv_c = conv_w.shape[2]
    alpha = (2.0 * depth) ** 0.25
    assert n_heads <= LANES and LANES % head_dim == 0 and att_w % LANES == 0

    tm = _tile(seq, 512)
    ta = _tile(seq, 1024)
    xf = x.reshape(t, d)
    for l in range(depth):
        mod = _mod(c, w_mod[l], b_mod[l])
        sh1, sc1, g1, sh2, sc2, g2 = [m.reshape(bsz, 1, d) for m in jnp.split(mod, 6, axis=-1)]

        o_f = 3 * att_w
        o_c = o_f + n_heads
        o_g = o_c + 2 * conv_c
        wl = w_in[l]
        wqkv = wl[:, :o_f].astype(BF16)
        wf = jnp.pad(wl[:, o_f:o_c], ((0, 0), (0, LANES - n_heads))).astype(BF16)
        wc = wl[:, o_c:o_g].astype(BF16)
        wg = wl[:, o_g:].astype(BF16)
        qkv, f, h, gates = _inproj(xf, sh1, sc1, wqkv, wf, wc, wg, seq=seq, att_w=att_w,
                                   conv_c=conv_c, head_dim=head_dim, tm=tm)

        bf_row = jnp.pad(b_f[l], (0, LANES - n_heads)).reshape(1, LANES)
        qb, kb = _fcum(f.reshape(bsz, seq, LANES), bf_row, n_heads=n_heads, tc=_tile(seq, 256))
        attn = _attention(qkv, qb.reshape(t, -1), kb.reshape(t, -1), bsz=bsz, seq=seq,
                          att_w=att_w, head_dim=head_dim, tq=ta, tk=ta)

        hc = _conv(h, conv_w[l], conv_b[l], conv_ln_g[l], conv_ln_b[l], seq=seq, ts=tm)

        x1, u2, qp = _merge(attn, hc, gates, xf, g1, sh2, sc2,
                            w_attn_out[l].astype(BF16), w_conv_out[l].astype(BF16),
                            w_out[l].astype(BF16), ln1_g[l], ln1_b[l],
                            peer_wq[l].astype(BF16), seq=seq, alpha=alpha, tm=tm)

        ii_t, jj_t, gg_t = _retrieve(qp, peer_sub_keys[l].astype(BF16), tl=tm)
        ii, jj, gg = ii_t.T, jj_t.T, gg_t.T
        a = _peer_act(u2, peer_u[l].astype(BF16), ii, jj, tb=tm, groups=32)
        xf = _peer_out(a, gg, ii, jj, peer_v[l].astype(BF16), x1, g2, ln2_g[l], ln2_b[l],
                       seq=seq, alpha=alpha, tb=tm, groups=8)

    return xf.reshape(bsz, seq, d)
```

```python
import functools
import math

import jax
import jax.numpy as jnp
import numpy as np
from jax import lax
from jax.experimental import pallas as pl
from jax.experimental.pallas import tpu as pltpu

F32 = jnp.float32
BF16 = jnp.bfloat16
I32 = jnp.int32

LN_EPS = 1e-5
PEER_TOPK = 16
LANES = 128
VMEM_LIMIT_BYTES = 56 * 1024 * 1024
NEG_INF = float("-inf")
LOG2_E = math.log2(math.e)


def _params(*semantics):
    return pltpu.CompilerParams(dimension_semantics=semantics, vmem_limit_bytes=VMEM_LIMIT_BYTES)


def _normalize(x):
    mu = jnp.mean(x, axis=-1, keepdims=True)
    xc = x - mu
    var = jnp.mean(xc * xc, axis=-1, keepdims=True)
    return xc * lax.rsqrt(var + LN_EPS)


def _resident(shape):
    nd = len(shape)
    return pl.BlockSpec(shape, lambda *_: (0,) * nd)


def _mod_kernel(c_ref, w_ref, b_ref, o_ref):
    c = c_ref[...]
    cs = c * jax.nn.sigmoid(c)
    o_ref[...] = jnp.dot(cs, w_ref[...], preferred_element_type=F32,
                         precision=lax.Precision.HIGHEST) + b_ref[...]


def _mod(c, w, b):
    bsz, d = c.shape
    n = w.shape[1]
    rows = -(-bsz // 8) * 8
    tn = math.gcd(n, 1536)
    cp = jnp.pad(c, ((0, rows - bsz), (0, 0)))
    out = pl.pallas_call(
        _mod_kernel,
        grid=(n // tn,),
        in_specs=[pl.BlockSpec((rows, d), lambda j: (0, 0)),
                  pl.BlockSpec((d, tn), lambda j: (0, j)),
                  pl.BlockSpec((1, tn), lambda j: (0, j))],
        out_specs=pl.BlockSpec((rows, tn), lambda j: (0, j)),
        out_shape=jax.ShapeDtypeStruct((rows, n), F32),
        compiler_params=_params("arbitrary"),
        name="mod",
    )(cp, w, b.reshape(1, n))
    return out[:bsz]


def _inproj_kernel(x_ref, sh_ref, sc_ref, wqkv_ref, wf_ref, wc_ref, wg_ref,
                   qkv_ref, f_ref, h_ref, g_ref, *, att_w, conv_c, qscale):
    y = _normalize(x_ref[...])
    u = (y * (1.0 + sc_ref[...]) + sh_ref[...]).astype(BF16)
    qkv = jnp.dot(u, wqkv_ref[...], preferred_element_type=F32)
    qkv_ref[:, :att_w] = (qkv[:, :att_w] * qscale).astype(BF16)
    qkv_ref[:, att_w:] = qkv[:, att_w:].astype(BF16)
    f_ref[...] = jnp.dot(u, wf_ref[...], preferred_element_type=F32)
    cab = jnp.dot(u, wc_ref[...], preferred_element_type=F32)
    h_ref[...] = cab[:, :conv_c] * jax.nn.sigmoid(cab[:, conv_c:])
    g_ref[...] = jax.nn.sigmoid(jnp.dot(u, wg_ref[...], preferred_element_type=F32))


def _inproj(x2, sh, sc, wqkv, wf, wc, wg, *, seq, att_w, conv_c, head_dim, tm):
    t, d = x2.shape
    tpb = seq // tm
    row = lambda i: (i, 0)
    per_batch = pl.BlockSpec((None, 1, d), lambda i: (i // tpb, 0, 0))
    kern = functools.partial(_inproj_kernel, att_w=att_w, conv_c=conv_c,
                             qscale=LOG2_E / math.sqrt(head_dim))
    return pl.pallas_call(
        kern,
        grid=(t // tm,),
        in_specs=[pl.BlockSpec((tm, d), row), per_batch, per_batch,
                  _resident(wqkv.shape), _resident(wf.shape), _resident(wc.shape),
                  _resident(wg.shape)],
        out_specs=[pl.BlockSpec((tm, 3 * att_w), row), pl.BlockSpec((tm, LANES), row),
                   pl.BlockSpec((tm, conv_c), row), pl.BlockSpec((tm, 2 * d), row)],
        out_shape=[jax.ShapeDtypeStruct((t, 3 * att_w), BF16),
                   jax.ShapeDtypeStruct((t, LANES), F32),
                   jax.ShapeDtypeStruct((t, conv_c), F32),
                   jax.ShapeDtypeStruct((t, 2 * d), F32)],
        compiler_params=_params("arbitrary"),
        name="inproj",
    )(x2, sh, sc, wqkv, wf, wc, wg)


BIAS_TERMS = 3


def _fcum_kernel(f_ref, bf_ref, qb_ref, kb_ref, carry_ref, *, tc, n_heads):
    @pl.when(pl.program_id(1) == 0)
    def _():
        carry_ref[...] = jnp.zeros(carry_ref.shape, F32)

    z = f_ref[...] + bf_ref[...]
    logf = jnp.minimum(z, 0.0) - jnp.log1p(jnp.exp(-jnp.abs(z)))
    r = lax.broadcasted_iota(I32, (tc, tc), 0)
    c = lax.broadcasted_iota(I32, (tc, tc), 1)
    tri = jnp.where(c <= r, 1.0, 0.0).astype(F32)
    cs = jnp.dot(tri, logf, preferred_element_type=F32,
                 precision=lax.Precision.HIGHEST) + carry_ref[...]
    carry_ref[...] = cs[tc - 1:tc, :]
    rem = cs * LOG2_E
    terms = []
    for _ in range(BIAS_TERMS):
        term = rem.astype(BF16)
        terms.append(term.astype(F32))
        rem = rem - terms[-1]
    lane = lax.broadcasted_iota(I32, (tc, LANES), 1)
    for hh in range(n_heads):
        qb = jnp.where((lane >= BIAS_TERMS) & (lane < 2 * BIAS_TERMS), -1.0, 0.0)
        kb = jnp.where(lane < BIAS_TERMS, 1.0, 0.0)
        for r, term in enumerate(terms):
            col = jnp.broadcast_to(term[:, hh:hh + 1], (tc, LANES))
            qb = jnp.where(lane == r, col, qb)
            kb = jnp.where(lane == BIAS_TERMS + r, col, kb)
        qb_ref[:, hh * LANES:(hh + 1) * LANES] = qb.astype(BF16)
        kb_ref[:, hh * LANES:(hh + 1) * LANES] = kb.astype(BF16)


def _fcum(f3, bf_row, *, n_heads, tc):
    bsz, seq, w = f3.shape
    slab = pl.BlockSpec((None, tc, n_heads * LANES), lambda b, j: (b, j, 0))
    out = jax.ShapeDtypeStruct((bsz, seq, n_heads * LANES), BF16)
    return pl.pallas_call(
        functools.partial(_fcum_kernel, tc=tc, n_heads=n_heads),
        grid=(bsz, seq // tc),
        in_specs=[pl.BlockSpec((None, tc, w), lambda b, j: (b, j, 0)),
                  pl.BlockSpec((1, w), lambda b, j: (0, 0))],
        out_specs=[slab, slab],
        out_shape=[out, out],
        scratch_shapes=[pltpu.VMEM((1, w), F32)],
        compiler_params=_params("arbitrary", "arbitrary"),
        name="fcum",
    )(f3, bf_row)


ATTN_MASKED = 1
ATTN_LAST = 2


def _attn_tables(nq, tq, tk):
    qi_l, ki_l, fl_l = [], [], []
    for qi in range(nq):
        last = (qi * tq + tq - 1) // tk
        for ki in range(last + 1):
            masked = ki * tk + tk - 1 > qi * tq
            qi_l.append(qi)
            ki_l.append(ki)
            fl_l.append((ATTN_MASKED if masked else 0) | (ATTN_LAST if ki == last else 0))
    as_i32 = lambda v: jnp.asarray(np.asarray(v, np.int32))
    return as_i32(qi_l), as_i32(ki_l), as_i32(fl_l)


def _attn_kernel(qi_tab, ki_tab, fl_tab, q_ref, qb_ref, k_ref, kb_ref, v_ref, o_ref,
                 qop_sc, m_sc, l_sc, acc_sc, *, tq, tk, rb, dh, heads):
    p = pl.program_id(2)
    qi = qi_tab[p]
    ki = ki_tab[p]
    fl = fl_tab[p]
    head_of_lane = lax.broadcasted_iota(I32, (tq, LANES), 1) // dh

    @pl.when(ki == 0)
    def _():
        m_sc[...] = jnp.full(m_sc.shape, NEG_INF, F32)
        l_sc[...] = jnp.zeros(l_sc.shape, F32)
        acc_sc[...] = jnp.zeros(acc_sc.shape, F32)
        q = q_ref[...]
        for hh in range(heads):
            qop_sc[hh, :, :LANES] = jnp.where(head_of_lane == hh, q, jnp.zeros_like(q))
            qop_sc[hh, :, LANES:] = qb_ref[:, hh * LANES:(hh + 1) * LANES]

    def per_head(vals):
        out = vals[0]
        lane_head = lax.broadcasted_iota(I32, out.shape, 1) // dh
        for hh in range(1, heads):
            out = jnp.where(lane_head == hh, vals[hh], out)
        return out

    def step(diagonal):
        for r0 in range(0, tq, rb):
            rows = slice(r0, r0 + rb)
            nk = min(tk, r0 + rb) if diagonal else tk
            if diagonal:
                keep = (lax.broadcasted_iota(I32, (rb, nk), 1)
                        <= r0 + lax.broadcasted_iota(I32, (rb, nk), 0))
            alphas, pvs = [], []
            for hh in range(heads):
                kop = jnp.concatenate([k_ref[:nk, :], kb_ref[:nk, hh * LANES:(hh + 1) * LANES]],
                                      axis=1)
                s = lax.dot_general(qop_sc[hh, rows, :], kop, (((1,), (1,)), ((), ())),
                                    preferred_element_type=F32)
                if diagonal:
                    s = jnp.where(keep, s, NEG_INF)
                m_prev = m_sc[hh, rows, :]
                m_new = jnp.maximum(m_prev, jnp.max(s, axis=-1, keepdims=True))
                alpha = jnp.exp2(m_prev - m_new)
                pr = jnp.exp2(s - jnp.concatenate([m_new] * (nk // LANES), axis=1))
                l_sc[hh, rows, :] = alpha * l_sc[hh, rows, :] + jnp.sum(pr, axis=-1, keepdims=True)
                m_sc[hh, rows, :] = m_new
                alphas.append(alpha)
                pvs.append(jnp.dot(pr.astype(BF16), v_ref[:nk, :], preferred_element_type=F32))
            acc_sc[rows, :] = per_head(alphas) * acc_sc[rows, :] + per_head(pvs)

    pl.when((fl & ATTN_MASKED) != 0)(lambda: step(True))
    pl.when((fl & ATTN_MASKED) == 0)(lambda: step(False))

    @pl.when((fl & ATTN_LAST) != 0)
    def _():
        o_ref[...] = (acc_sc[...] / per_head([l_sc[hh] for hh in range(heads)])).astype(o_ref.dtype)


def _attention(qkv, qb, kb, *, bsz, seq, att_w, head_dim, tq, tk, rb):
    heads = LANES // head_dim
    ncol = att_w // LANES
    nq, nk = seq // tq, seq // tk
    assert tq == tk and tq % rb == 0 and rb % LANES == 0
    qi_tab, ki_tab, fl_tab = _attn_tables(nq, tq, tk)
    kern = functools.partial(_attn_kernel, tq=tq, tk=tk, rb=rb, dh=head_dim, heads=heads)
    qrow = lambda b, c, p, qt, kt, ft: b * nq + qt[p]
    krow = lambda b, c, p, qt, kt, ft: b * nk + kt[p]
    grid_spec = pltpu.PrefetchScalarGridSpec(
        num_scalar_prefetch=3,
        grid=(bsz, ncol, int(qi_tab.shape[0])),
        in_specs=[
            pl.BlockSpec((tq, LANES), lambda b, c, *a: (qrow(b, c, *a), c)),
            pl.BlockSpec((tq, heads * LANES), lambda b, c, *a: (qrow(b, c, *a), c)),
            pl.BlockSpec((tk, LANES), lambda b, c, *a: (krow(b, c, *a), ncol + c)),
            pl.BlockSpec((tk, heads * LANES), lambda b, c, *a: (krow(b, c, *a), c)),
            pl.BlockSpec((tk, LANES), lambda b, c, *a: (krow(b, c, *a), 2 * ncol + c)),
        ],
        out_specs=pl.BlockSpec((tq, LANES), lambda b, c, *a: (qrow(b, c, *a), c)),
        scratch_shapes=[pltpu.VMEM((heads, tq, 2 * LANES), BF16),
                        pltpu.VMEM((heads, tq, LANES), F32), pltpu.VMEM((heads, tq, LANES), F32),
                        pltpu.VMEM((tq, LANES), F32)],
    )
    return pl.pallas_call(
        kern,
        grid_spec=grid_spec,
        out_shape=jax.ShapeDtypeStruct((bsz * seq, att_w), BF16),
        compiler_params=_params("arbitrary", "arbitrary", "arbitrary"),
        name="attn",
    )(qi_tab, ki_tab, fl_tab, qkv, qb, qkv, kb, qkv)


CONV_HALO = 32
CONV_CHUNK = 64


SUBLANES = 8


def _conv_kernel(h_ref, halo_ref, w_ref, b_ref, g_ref, beta_ref, o_ref, sh_sc, *, ts, taps, tpb):
    first = pl.program_id(0) % tpb == 0
    rows = ts + CONV_HALO - SUBLANES
    sh_sc[0, 0:CONV_HALO, :] = jnp.where(first, 0.0, halo_ref[...])
    sh_sc[0, CONV_HALO:, :] = h_ref[...]
    for r in range(1, SUBLANES):
        sh_sc[r, 0:rows, :] = sh_sc[0, r:r + rows, :]
    lead = CONV_HALO - (taps - 1)
    for c0 in range(0, ts, CONV_CHUNK):
        acc = jnp.broadcast_to(b_ref[...], (CONV_CHUNK, b_ref.shape[1]))
        for k in range(taps):
            r = (lead + k) % SUBLANES
            start = c0 + lead + k - r
            acc = acc + w_ref[k:k + 1, :] * sh_sc[r, start:start + CONV_CHUNK, :]
        y = _normalize(acc) * g_ref[...] + beta_ref[...]
        o_ref[c0:c0 + CONV_CHUNK, :] = (y * jax.nn.sigmoid(y)).astype(o_ref.dtype)


def _conv(h, w, b, g, beta, *, seq, ts):
    t, c = h.shape
    taps = w.shape[0]
    assert taps - 1 <= CONV_HALO and ts % CONV_CHUNK == 0 and ts % CONV_HALO == 0
    tpb = seq // ts
    hb = ts // CONV_HALO
    vec = lambda v: v.reshape(1, c)
    return pl.pallas_call(
        functools.partial(_conv_kernel, ts=ts, taps=taps, tpb=tpb),
        grid=(t // ts,),
        in_specs=[pl.BlockSpec((ts, c), lambda i: (i, 0)),
                  pl.BlockSpec((CONV_HALO, c), lambda i: (jnp.maximum(i * hb - 1, 0), 0)),
                  _resident((taps, c)), _resident((1, c)), _resident((1, c)), _resident((1, c))],
        out_specs=pl.BlockSpec((ts, c), lambda i: (i, 0)),
        out_shape=jax.ShapeDtypeStruct((t, c), BF16),
        scratch_shapes=[pltpu.VMEM((SUBLANES, ts + CONV_HALO, c), F32)],
        compiler_params=_params("arbitrary"),
        name="conv",
    )(h, h, w, vec(b), vec(g), vec(beta))


def _merge_kernel(attn_ref, hc_ref, g_ref, x_ref, g1_ref, sh2_ref, sc2_ref, wa_ref, wc_ref,
                  wo_ref, lg_ref, lb_ref, wq_ref, x1_ref, u2_ref, qp_ref, *, alpha):
    d = x_ref.shape[1]
    ya = jnp.dot(attn_ref[...], wa_ref[...], preferred_element_type=F32)
    yb = jnp.dot(hc_ref[...], wc_ref[...], preferred_element_type=F32)
    merged = (g_ref[:, :d] * ya + g_ref[:, d:] * yb).astype(BF16)
    out = jnp.dot(merged, wo_ref[...], preferred_element_type=F32)
    x1 = _normalize(alpha * x_ref[...] + (1.0 + g1_ref[...]) * out) * lg_ref[...] + lb_ref[...]
    x1_ref[...] = x1
    u2 = (_normalize(x1) * (1.0 + sc2_ref[...]) + sh2_ref[...]).astype(BF16)
    u2_ref[...] = u2
    qp_ref[...] = jnp.dot(u2, wq_ref[...], preferred_element_type=F32).astype(qp_ref.dtype)


def _merge(attn, hc, gates, x2, g1, sh2, sc2, wa, wc, wo, lg, lb, wq, *, seq, alpha, tm):
    t, d = x2.shape
    tpb = seq // tm
    row = lambda i: (i, 0)
    per_batch = pl.BlockSpec((None, 1, d), lambda i: (i // tpb, 0, 0))
    nq = wq.shape[1]
    return pl.pallas_call(
        functools.partial(_merge_kernel, alpha=alpha),
        grid=(t // tm,),
        in_specs=[pl.BlockSpec((tm, attn.shape[1]), row), pl.BlockSpec((tm, hc.shape[1]), row),
                  pl.BlockSpec((tm, 2 * d), row), pl.BlockSpec((tm, d), row),
                  per_batch, per_batch, per_batch,
                  _resident(wa.shape), _resident(wc.shape), _resident(wo.shape),
                  _resident((1, d)), _resident((1, d)), _resident(wq.shape)],
        out_specs=[pl.BlockSpec((tm, d), row), pl.BlockSpec((tm, d), row),
                   pl.BlockSpec((tm, nq), row)],
        out_shape=[jax.ShapeDtypeStruct((t, d), F32), jax.ShapeDtypeStruct((t, d), BF16),
                   jax.ShapeDtypeStruct((t, nq), BF16)],
        compiler_params=_params("arbitrary"),
        name="merge",
    )(attn, hc, gates, x2, g1, sh2, sc2, wa, wc, wo, lg.reshape(1, d), lb.reshape(1, d), wq)


def _staircase(k):
    return [(a, b) for a in range(k) for b in range(k) if (a + 1) * (b + 1) <= k]


ID_PAD = 1e9


def _extract_topk(s, ids, k):
    vals, idxs = [], []
    for _ in range(k):
        m = jnp.max(s, axis=0, keepdims=True)
        i = jnp.min(jnp.where(s == m, ids, ID_PAD), axis=0, keepdims=True)
        s = jnp.where(ids == i, NEG_INF, s)
        vals.append(m)
        idxs.append(i)
    return jnp.concatenate(vals, axis=0), jnp.concatenate(idxs, axis=0)


RETRIEVE_HEADS = 2


def _retrieve_kernel(qp_ref, keys_ref, flat_ref, ii_ref, jj_ref, gg_ref, *, dhalf, nkeys, topk):
    tl = qp_ref.shape[0]
    qb = qp_ref[...]
    key_ids = lax.broadcasted_iota(I32, (nkeys, tl), 0).astype(F32)
    pairs = _staircase(topk)
    npad = flat_ref.shape[0]
    rank = lax.broadcasted_iota(I32, (topk, tl), 0)
    for hh in range(RETRIEVE_HEADS):
        sv, si = [], []
        for half in range(2):
            col = (2 * hh + half) * dhalf
            s = lax.dot_general(keys_ref[half], qb[:, col:col + dhalf],
                                (((1,), (1,)), ((), ())), preferred_element_type=F32)
            v, i = _extract_topk(s, key_ids, topk)
            sv.append(v)
            si.append(i)

        cand = [sv[0][a:a + 1, :] + sv[1][b:b + 1, :] for a, b in pairs]
        cand += [jnp.full((1, tl), NEG_INF, F32)] * (npad - len(pairs))
        cand = jnp.concatenate(cand, axis=0)
        cv, ci = _extract_topk(cand, flat_ref[...], topk)

        ci = ci.astype(I32)
        ia = lax.shift_right_logical(ci, int(math.log2(topk)))
        ib = ci & (topk - 1)
        ii_rows, jj_rows = [], []
        for r in range(topk):
            ii_rows.append(jnp.sum(jnp.where(rank == ia[r:r + 1, :], si[0], 0.0), axis=0,
                                   keepdims=True))
            jj_rows.append(jnp.sum(jnp.where(rank == ib[r:r + 1, :], si[1], 0.0), axis=0,
                                   keepdims=True))
        rows = slice(hh * topk, (hh + 1) * topk)
        ii_ref[rows, :] = jnp.concatenate(ii_rows, axis=0).astype(I32)
        jj_ref[rows, :] = jnp.concatenate(jj_rows, axis=0).astype(I32)
        e = jnp.exp(cv - cv[0:1, :])
        gg_ref[rows, :] = e / jnp.sum(e, axis=0, keepdims=True)


def _retrieve(qp, keys, *, tl):
    t, nq = qp.shape
    _, nkeys, dhalf = keys.shape
    heads = nq // (2 * dhalf)
    assert 2 * dhalf == LANES and PEER_TOPK & (PEER_TOPK - 1) == 0 and heads % RETRIEVE_HEADS == 0
    pairs = _staircase(PEER_TOPK)
    npad = -(-len(pairs) // 8) * 8
    flat = np.full((npad, tl), ID_PAD / 2, np.float32)
    flat[:len(pairs), :] = np.asarray([a * PEER_TOPK + b for a, b in pairs], np.float32)[:, None]
    kern = functools.partial(_retrieve_kernel, dhalf=dhalf, nkeys=nkeys, topk=PEER_TOPK)
    out = jax.ShapeDtypeStruct((heads * PEER_TOPK, t), I32)
    blk = pl.BlockSpec((RETRIEVE_HEADS * PEER_TOPK, tl), lambda i, h: (h, i))
    return pl.pallas_call(
        kern,
        grid=(t // tl, heads // RETRIEVE_HEADS),
        in_specs=[pl.BlockSpec((tl, RETRIEVE_HEADS * LANES), lambda i, h: (i, h)),
                  _resident(keys.shape),
                  _resident(flat.shape)],
        out_specs=[blk, blk, blk],
        out_shape=[out, out, jax.ShapeDtypeStruct((heads * PEER_TOPK, t), F32)],
        compiler_params=_params("arbitrary", "arbitrary"),
        name="retrieve",
    )(qp, keys, jnp.asarray(flat))


def _gelu_tanh(a):
    return 0.5 * a * (1.0 + jnp.tanh(math.sqrt(2.0 / math.pi) * (a + 0.044715 * a * a * a)))


def _peer_act_kernel(x_ref, u_ref, ii_ref, jj_ref, a_ref, *, groups):
    c = pl.program_id(1)

    @pl.when(c == 0)
    def _():
        a_ref[...] = jnp.zeros(a_ref.shape, F32)

    a_all = lax.dot_general(x_ref[...], u_ref[...], (((1,), (1,)), ((), ())),
                            preferred_element_type=F32)
    ii = ii_ref[...]
    jj = jj_ref[...]
    acc = a_ref[...]
    for g in range(groups):
        picked = jnp.take_along_axis(a_all[:, g * LANES:(g + 1) * LANES], jj, axis=1)
        acc = jnp.where(ii == c * groups + g, picked, acc)
    a_ref[...] = acc


def _peer_act(u2, utab, ii, jj, *, tb, groups):
    t, d = u2.shape
    ne = utab.shape[0]
    ce = groups * LANES
    tok = lambda i, c: (i, 0)
    return pl.pallas_call(
        functools.partial(_peer_act_kernel, groups=groups),
        grid=(t // tb, ne // ce),
        in_specs=[pl.BlockSpec((tb, d), tok), pl.BlockSpec((ce, d), lambda i, c: (c, 0)),
                  pl.BlockSpec((tb, LANES), tok), pl.BlockSpec((tb, LANES), tok)],
        out_specs=pl.BlockSpec((tb, LANES), tok),
        out_shape=jax.ShapeDtypeStruct((t, LANES), F32),
        compiler_params=_params("arbitrary", "arbitrary"),
        name="peer_act",
    )(u2, utab, ii, jj)


SCATTER_UNROLL = 32


def _w_pitch(tb):
    p = -(-tb // 8)
    return 8 * (p if p % 2 else p + 1)


def _peer_out_kernel(a_ref, gg_ref, ii_ref, jj_ref, v_ref, x1_ref, g2_ref, lg_ref, lb_ref,
                     o_ref, w_sc, acc_sc, *, tb, groups, nchunks, alpha):
    c = pl.program_id(1)
    pitch = _w_pitch(tb)

    @pl.when(c == 0)
    def _():
        acc_sc[...] = jnp.zeros(acc_sc.shape, F32)
        sub = lax.broadcasted_iota(I32, (LANES, LANES), 0)

        def scatter(tok, carry):
            wrow = _gelu_tanh(a_ref[pl.ds(tok, 1), :]) * gg_ref[pl.ds(tok, 1), :]
            lhs = jnp.where(sub == ii_ref[pl.ds(tok, 1), :], wrow, 0.0).astype(BF16)
            rhs = jnp.where(sub == jj_ref[pl.ds(tok, 1), :], 1.0, 0.0).astype(BF16)
            w_sc[pl.ds(tok, LANES, stride=pitch), :] = lax.dot_general(
                lhs, rhs, (((1,), (1,)), ((), ())), preferred_element_type=F32)
            return carry

        lax.fori_loop(0, tb, scatter, 0, unroll=SCATTER_UNROLL)

    parts = [w_sc[pl.ds(pl.multiple_of((c * groups + g) * pitch, 8), tb), :].astype(BF16)
             for g in range(groups)]
    acc_sc[...] += jnp.dot(jnp.concatenate(parts, axis=1), v_ref[...],
                           preferred_element_type=F32)

    @pl.when(c == nchunks - 1)
    def _():
        r = alpha * x1_ref[...] + (1.0 + g2_ref[...]) * acc_sc[...]
        o_ref[...] = _normalize(r) * lg_ref[...] + lb_ref[...]


def _peer_out(a, gg, ii, jj, vtab, x1, g2, lg, lb, *, seq, alpha, tb, groups):
    t, d = x1.shape
    ne = vtab.shape[0]
    assert ne == LANES * LANES
    ce = groups * LANES
    nchunks = ne // ce
    tpb = seq // tb
    tok = lambda i, c: (i, 0)
    kern = functools.partial(_peer_out_kernel, tb=tb, groups=groups, nchunks=nchunks, alpha=alpha)
    return pl.pallas_call(
        kern,
        grid=(t // tb, nchunks),
        in_specs=[pl.BlockSpec((tb, LANES), tok), pl.BlockSpec((tb, LANES), tok),
                  pl.BlockSpec((tb, LANES), tok), pl.BlockSpec((tb, LANES), tok),
                  pl.BlockSpec((ce, d), lambda i, c: (c, 0)),
                  pl.BlockSpec((tb, d), tok, pipeline_mode=pl.Buffered(1)),
                  pl.BlockSpec((None, 1, d), lambda i, c: (i // tpb, 0, 0)),
                  _resident((1, d)), _resident((1, d))],
        out_specs=pl.BlockSpec((tb, d), tok),
        out_shape=jax.ShapeDtypeStruct((t, d), F32),
        scratch_shapes=[pltpu.VMEM((LANES * _w_pitch(tb), LANES), F32), pltpu.VMEM((tb, d), F32)],
        compiler_params=_params("arbitrary", "arbitrary"),
        name="peer_out",
    )(a, gg, ii, jj, vtab, x1, g2, lg.reshape(1, d), lb.reshape(1, d))


def _tile(n, want):
    t = min(n, want)
    while n % t:
        t -= 1
    return t


def kernel(x, c, w_mod, b_mod, w_in, b_f, conv_w, conv_b, conv_ln_g, conv_ln_b,
           w_attn_out, w_conv_out, w_out, ln1_g, ln1_b, peer_wq, peer_sub_keys,
           peer_u, peer_v, ln2_g, ln2_b):
    bsz, seq, d = x.shape
    depth = w_mod.shape[0]
    t = bsz * seq
    att_w = w_attn_out.shape[1]
    n_heads = b_f.shape[1]
    head_dim = att_w // n_heads
    conv_c = conv_w.shape[2]
    alpha = (2.0 * depth) ** 0.25
    assert n_heads <= LANES and LANES % head_dim == 0 and att_w % LANES == 0

    tm = _tile(seq, 512)
    ta = _tile(seq, 2048)
    tp = _tile(seq, 1024)
    xf = x.reshape(t, d)
    for l in range(depth):
        mod = _mod(c, w_mod[l], b_mod[l])
        sh1, sc1, g1, sh2, sc2, g2 = [m.reshape(bsz, 1, d) for m in jnp.split(mod, 6, axis=-1)]

        o_f = 3 * att_w
        o_c = o_f + n_heads
        o_g = o_c + 2 * conv_c
        wl = w_in[l]
        wqkv = wl[:, :o_f].astype(BF16)
        wf = jnp.pad(wl[:, o_f:o_c], ((0, 0), (0, LANES - n_heads))).astype(BF16)
        wc = wl[:, o_c:o_g].astype(BF16)
        wg = wl[:, o_g:].astype(BF16)
        qkv, f, h, gates = _inproj(xf, sh1, sc1, wqkv, wf, wc, wg, seq=seq, att_w=att_w,
                                   conv_c=conv_c, head_dim=head_dim, tm=tm)

        bf_row = jnp.pad(b_f[l], (0, LANES - n_heads)).reshape(1, LANES)
        qb, kb = _fcum(f.reshape(bsz, seq, LANES), bf_row, n_heads=n_heads, tc=_tile(seq, 256))
        attn = _attention(qkv, qb.reshape(t, -1), kb.reshape(t, -1), bsz=bsz, seq=seq,
                          att_w=att_w, head_dim=head_dim, tq=ta, tk=ta, rb=_tile(ta, 512))

        hc = _conv(h, conv_w[l], conv_b[l], conv_ln_g[l], conv_ln_b[l], seq=seq, ts=tm)

        x1, u2, qp = _merge(attn, hc, gates, xf, g1, sh2, sc2,
                            w_attn_out[l].astype(BF16), w_conv_out[l].astype(BF16),
                            w_out[l].astype(BF16), ln1_g[l], ln1_b[l],
                            peer_wq[l].astype(BF16), seq=seq, alpha=alpha, tm=tm)

        ii_t, jj_t, gg_t = _retrieve(qp, peer_sub_keys[l].astype(BF16), tl=tm)
        ii, jj, gg = ii_t.T, jj_t.T, gg_t.T
        a = _peer_act(u2, peer_u[l].astype(BF16), ii, jj, tb=tp, groups=32)
        xf = _peer_out(a, gg, ii, jj, peer_v[l].astype(BF16), x1, g2, ln2_g[l], ln2_b[l],
                       seq=seq, alpha=alpha, tb=tm, groups=16)
    return xf.reshape(bsz, seq, d)
```

```python
import functools
import math

import jax
import jax.numpy as jnp
import numpy as np
from jax import lax
from jax.experimental import pallas as pl
from jax.experimental.pallas import tpu as pltpu

F32 = jnp.float32
BF16 = jnp.bfloat16
I32 = jnp.int32

LN_EPS = 1e-5
PEER_TOPK = 16
LANES = 128
VMEM_LIMIT_BYTES = 56 * 1024 * 1024
NEG_INF = float("-inf")
LOG2_E = math.log2(math.e)


def _params(*semantics):
    return pltpu.CompilerParams(dimension_semantics=semantics, vmem_limit_bytes=VMEM_LIMIT_BYTES)


def _normalize(x):
    mu = jnp.mean(x, axis=-1, keepdims=True)
    xc = x - mu
    var = jnp.mean(xc * xc, axis=-1, keepdims=True)
    return xc * lax.rsqrt(var + LN_EPS)


def _resident(shape):
    nd = len(shape)
    return pl.BlockSpec(shape, lambda *_: (0,) * nd)


def _mod_kernel(c_ref, w_ref, b_ref, o_ref):
    c = c_ref[...]
    cs = c * jax.nn.sigmoid(c)
    o_ref[...] = jnp.dot(cs, w_ref[...], preferred_element_type=F32,
                         precision=lax.Precision.HIGHEST) + b_ref[...]


def _mod(c, w, b):
    bsz, d = c.shape
    n = w.shape[1]
    rows = -(-bsz // 8) * 8
    tn = math.gcd(n, 1536)
    cp = jnp.pad(c, ((0, rows - bsz), (0, 0)))
    out = pl.pallas_call(
        _mod_kernel,
        grid=(n // tn,),
        in_specs=[pl.BlockSpec((rows, d), lambda j: (0, 0)),
                  pl.BlockSpec((d, tn), lambda j: (0, j)),
                  pl.BlockSpec((1, tn), lambda j: (0, j))],
        out_specs=pl.BlockSpec((rows, tn), lambda j: (0, j)),
        out_shape=jax.ShapeDtypeStruct((rows, n), F32),
        compiler_params=_params("arbitrary"),
        name="mod",
    )(cp, w, b.reshape(1, n))
    return out[:bsz]


def _inproj_kernel(x_ref, sh_ref, sc_ref, wqkv_ref, wf_ref, wc_ref, wg_ref,
                   qkv_ref, f_ref, h_ref, g_ref, *, att_w, conv_c, qscale):
    y = _normalize(x_ref[...])
    u = (y * (1.0 + sc_ref[...]) + sh_ref[...]).astype(BF16)
    qkv = jnp.dot(u, wqkv_ref[...], preferred_element_type=F32)
    qkv_ref[:, :att_w] = (qkv[:, :att_w] * qscale).astype(BF16)
    qkv_ref[:, att_w:] = qkv[:, att_w:].astype(BF16)
    f_ref[...] = jnp.dot(u, wf_ref[...], preferred_element_type=F32)
    cab = jnp.dot(u, wc_ref[...], preferred_element_type=F32)
    h_ref[...] = cab[:, :conv_c] * jax.nn.sigmoid(cab[:, conv_c:])
    g_ref[...] = jax.nn.sigmoid(jnp.dot(u, wg_ref[...], preferred_element_type=F32)
                                ).astype(g_ref.dtype)


def _inproj(x2, sh, sc, wqkv, wf, wc, wg, *, seq, att_w, conv_c, head_dim, tm):
    t, d = x2.shape
    tpb = seq // tm
    row = lambda i: (i, 0)
    per_batch = pl.BlockSpec((None, 1, d), lambda i: (i // tpb, 0, 0))
    kern = functools.partial(_inproj_kernel, att_w=att_w, conv_c=conv_c,
                             qscale=LOG2_E / math.sqrt(head_dim))
    return pl.pallas_call(
        kern,
        grid=(t // tm,),
        in_specs=[pl.BlockSpec((tm, d), row), per_batch, per_batch,
                  _resident(wqkv.shape), _resident(wf.shape), _resident(wc.shape),
                  _resident(wg.shape)],
        out_specs=[pl.BlockSpec((tm, 3 * att_w), row), pl.BlockSpec((tm, LANES), row),
                   pl.BlockSpec((tm, conv_c), row), pl.BlockSpec((tm, 2 * d), row)],
        out_shape=[jax.ShapeDtypeStruct((t, 3 * att_w), BF16),
                   jax.ShapeDtypeStruct((t, LANES), F32),
                   jax.ShapeDtypeStruct((t, conv_c), F32),
                   jax.ShapeDtypeStruct((t, 2 * d), BF16)],
        compiler_params=_params("arbitrary"),
        name="inproj",
    )(x2, sh, sc, wqkv, wf, wc, wg)


BIAS_TERMS = 3


def _bias_slab_maps(n_heads):
    pq = np.zeros((BIAS_TERMS * LANES, n_heads * LANES), np.float32)
    pk = np.zeros_like(pq)
    cq = np.zeros((1, n_heads * LANES), np.float32)
    ck = np.zeros_like(cq)
    for h in range(n_heads):
        for r in range(BIAS_TERMS):
            pq[r * LANES + h, h * LANES + r] = 1.0
            pk[r * LANES + h, h * LANES + BIAS_TERMS + r] = 1.0
            cq[0, h * LANES + BIAS_TERMS + r] = -1.0
            ck[0, h * LANES + r] = 1.0
    return (jnp.asarray(pq, BF16), jnp.asarray(pk, BF16), jnp.asarray(cq), jnp.asarray(ck))


def _fcum_kernel(f_ref, bf_ref, pq_ref, pk_ref, cq_ref, ck_ref, qb_ref, kb_ref, carry_ref, *, tc):
    @pl.when(pl.program_id(1) == 0)
    def _():
        carry_ref[...] = jnp.zeros(carry_ref.shape, F32)

    z = f_ref[...] + bf_ref[...]
    logf = jnp.minimum(z, 0.0) - jnp.log1p(jnp.exp(-jnp.abs(z)))
    r = lax.broadcasted_iota(I32, (tc, tc), 0)
    c = lax.broadcasted_iota(I32, (tc, tc), 1)
    tri = jnp.where(c <= r, 1.0, 0.0).astype(F32)
    cs = jnp.dot(tri, logf, preferred_element_type=F32,
                 precision=lax.Precision.HIGHEST) + carry_ref[...]
    carry_ref[...] = cs[tc - 1:tc, :]
    rem = cs * LOG2_E
    terms = []
    for _ in range(BIAS_TERMS):
        term = rem.astype(BF16)
        terms.append(term)
        rem = rem - term.astype(F32)
    terms = jnp.concatenate(terms, axis=1)
    qb_ref[...] = (jnp.dot(terms, pq_ref[...], preferred_element_type=F32)
                   + cq_ref[...]).astype(BF16)
    kb_ref[...] = (jnp.dot(terms, pk_ref[...], preferred_element_type=F32)
                   + ck_ref[...]).astype(BF16)


def _fcum(f3, bf_row, *, n_heads, tc):
    bsz, seq, w = f3.shape
    assert w == LANES
    slab = pl.BlockSpec((None, tc, n_heads * LANES), lambda b, j: (b, j, 0))
    out = jax.ShapeDtypeStruct((bsz, seq, n_heads * LANES), BF16)
    maps = _bias_slab_maps(n_heads)
    return pl.pallas_call(
        functools.partial(_fcum_kernel, tc=tc),
        grid=(bsz, seq // tc),
        in_specs=[pl.BlockSpec((None, tc, w), lambda b, j: (b, j, 0)),
                  pl.BlockSpec((1, w), lambda b, j: (0, 0))] + [_resident(m.shape) for m in maps],
        out_specs=[slab, slab],
        out_shape=[out, out],
        scratch_shapes=[pltpu.VMEM((1, w), F32)],
        compiler_params=_params("arbitrary", "arbitrary"),
        name="fcum",
    )(f3, bf_row, *maps)


ATTN_MASKED = 1
ATTN_LAST = 2


def _attn_tables(nq, tq, tk):
    qi_l, ki_l, fl_l = [], [], []
    for qi in range(nq):
        last = (qi * tq + tq - 1) // tk
        for ki in range(last + 1):
            masked = ki * tk + tk - 1 > qi * tq
            qi_l.append(qi)
            ki_l.append(ki)
            fl_l.append((ATTN_MASKED if masked else 0) | (ATTN_LAST if ki == last else 0))
    as_i32 = lambda v: jnp.asarray(np.asarray(v, np.int32))
    return as_i32(qi_l), as_i32(ki_l), as_i32(fl_l)


def _attn_kernel(qi_tab, ki_tab, fl_tab, q_ref, qb_ref, k_ref, kb_ref, v_ref, o_ref,
                 qop_sc, m_sc, l_sc, acc_sc, *, tq, tk, rb, dh, heads):
    p = pl.program_id(2)
    qi = qi_tab[p]
    ki = ki_tab[p]
    fl = fl_tab[p]
    head_of_lane = lax.broadcasted_iota(I32, (tq, LANES), 1) // dh

    @pl.when(ki == 0)
    def _():
        m_sc[...] = jnp.full(m_sc.shape, NEG_INF, F32)
        l_sc[...] = jnp.zeros(l_sc.shape, F32)
        acc_sc[...] = jnp.zeros(acc_sc.shape, F32)
        q = q_ref[...]
        for hh in range(heads):
            qop_sc[hh, :, :LANES] = jnp.where(head_of_lane == hh, q, jnp.zeros_like(q))
            qop_sc[hh, :, LANES:] = qb_ref[:, hh * LANES:(hh + 1) * LANES]

    def per_head(vals):
        out = vals[0]
        lane_head = lax.broadcasted_iota(I32, out.shape, 1) // dh
        for hh in range(1, heads):
            out = jnp.where(lane_head == hh, vals[hh], out)
        return out

    def step(diagonal):
        for r0 in range(0, tq, rb):
            rows = slice(r0, r0 + rb)
            nk = min(tk, r0 + rb) if diagonal else tk
            if diagonal:
                keep = (lax.broadcasted_iota(I32, (rb, nk), 1)
                        <= r0 + lax.broadcasted_iota(I32, (rb, nk), 0))
            alphas, pvs = [], []
            for hh in range(heads):
                kop = jnp.concatenate([k_ref[:nk, :], kb_ref[:nk, hh * LANES:(hh + 1) * LANES]],
                                      axis=1)
                s = lax.dot_general(qop_sc[hh, rows, :], kop, (((1,), (1,)), ((), ())),
                                    preferred_element_type=F32)
                if diagonal:
                    s = jnp.where(keep, s, NEG_INF)
                m_prev = m_sc[hh, rows, :]
                m_new = jnp.maximum(m_prev, jnp.max(s, axis=-1, keepdims=True))
                alpha = jnp.exp2(m_prev - m_new)
                pr = jnp.exp2(s - jnp.concatenate([m_new] * (nk // LANES), axis=1))
                l_sc[hh, rows, :] = alpha * l_sc[hh, rows, :] + jnp.sum(pr, axis=-1, keepdims=True)
                m_sc[hh, rows, :] = m_new
                alphas.append(alpha)
                pvs.append(jnp.dot(pr.astype(BF16), v_ref[:nk, :], preferred_element_type=F32))
            acc_sc[rows, :] = per_head(alphas) * acc_sc[rows, :] + per_head(pvs)

    pl.when((fl & ATTN_MASKED) != 0)(lambda: step(True))
    pl.when((fl & ATTN_MASKED) == 0)(lambda: step(False))

    @pl.when((fl & ATTN_LAST) != 0)
    def _():
        o_ref[...] = (acc_sc[...] / per_head([l_sc[hh] for hh in range(heads)])).astype(o_ref.dtype)


def _attention(qkv, qb, kb, *, bsz, seq, att_w, head_dim, tq, tk, rb):
    heads = LANES // head_dim
    ncol = att_w // LANES
    nq, nk = seq // tq, seq // tk
    assert tq == tk and tq % rb == 0 and rb % LANES == 0
    qi_tab, ki_tab, fl_tab = _attn_tables(nq, tq, tk)
    kern = functools.partial(_attn_kernel, tq=tq, tk=tk, rb=rb, dh=head_dim, heads=heads)
    qrow = lambda b, c, p, qt, kt, ft: b * nq + qt[p]
    krow = lambda b, c, p, qt, kt, ft: b * nk + kt[p]
    grid_spec = pltpu.PrefetchScalarGridSpec(
        num_scalar_prefetch=3,
        grid=(bsz, ncol, int(qi_tab.shape[0])),
        in_specs=[
            pl.BlockSpec((tq, LANES), lambda b, c, *a: (qrow(b, c, *a), c)),
            pl.BlockSpec((tq, heads * LANES), lambda b, c, *a: (qrow(b, c, *a), c)),
            pl.BlockSpec((tk, LANES), lambda b, c, *a: (krow(b, c, *a), ncol + c)),
            pl.BlockSpec((tk, heads * LANES), lambda b, c, *a: (krow(b, c, *a), c)),
            pl.BlockSpec((tk, LANES), lambda b, c, *a: (krow(b, c, *a), 2 * ncol + c)),
        ],
        out_specs=pl.BlockSpec((tq, LANES), lambda b, c, *a: (qrow(b, c, *a), c)),
        scratch_shapes=[pltpu.VMEM((heads, tq, 2 * LANES), BF16),
                        pltpu.VMEM((heads, tq, LANES), F32), pltpu.VMEM((heads, tq, LANES), F32),
                        pltpu.VMEM((tq, LANES), F32)],
    )
    return pl.pallas_call(
        kern,
        grid_spec=grid_spec,
        out_shape=jax.ShapeDtypeStruct((bsz * seq, att_w), BF16),
        compiler_params=_params("arbitrary", "arbitrary", "arbitrary"),
        name="attn",
    )(qi_tab, ki_tab, fl_tab, qkv, qb, qkv, kb, qkv)


CONV_HALO = 32
CONV_CHUNK = 64


SUBLANES = 8


def _conv_kernel(h_ref, halo_ref, w_ref, b_ref, g_ref, beta_ref, o_ref, sh_sc, *, ts, taps, tpb):
    first = pl.program_id(0) % tpb == 0
    rows = ts + CONV_HALO - SUBLANES
    sh_sc[0, 0:CONV_HALO, :] = jnp.where(first, 0.0, halo_ref[...])
    sh_sc[0, CONV_HALO:, :] = h_ref[...]
    for r in range(1, SUBLANES):
        sh_sc[r, 0:rows, :] = sh_sc[0, r:r + rows, :]
    lead = CONV_HALO - (taps - 1)
    for c0 in range(0, ts, CONV_CHUNK):
        acc = jnp.broadcast_to(b_ref[...], (CONV_CHUNK, b_ref.shape[1]))
        for k in range(taps):
            r = (lead + k) % SUBLANES
            start = c0 + lead + k - r
            acc = acc + w_ref[k:k + 1, :] * sh_sc[r, start:start + CONV_CHUNK, :]
        y = _normalize(acc) * g_ref[...] + beta_ref[...]
        o_ref[c0:c0 + CONV_CHUNK, :] = (y * jax.nn.sigmoid(y)).astype(o_ref.dtype)


def _conv(h, w, b, g, beta, *, seq, ts):
    t, c = h.shape
    taps = w.shape[0]
    assert taps - 1 <= CONV_HALO and ts % CONV_CHUNK == 0 and ts % CONV_HALO == 0
    tpb = seq // ts
    hb = ts // CONV_HALO
    vec = lambda v: v.reshape(1, c)
    return pl.pallas_call(
        functools.partial(_conv_kernel, ts=ts, taps=taps, tpb=tpb),
        grid=(t // ts,),
        in_specs=[pl.BlockSpec((ts, c), lambda i: (i, 0)),
                  pl.BlockSpec((CONV_HALO, c), lambda i: (jnp.maximum(i * hb - 1, 0), 0)),
                  _resident((taps, c)), _resident((1, c)), _resident((1, c)), _resident((1, c))],
        out_specs=pl.BlockSpec((ts, c), lambda i: (i, 0)),
        out_shape=jax.ShapeDtypeStruct((t, c), BF16),
        scratch_shapes=[pltpu.VMEM((SUBLANES, ts + CONV_HALO, c), F32)],
        compiler_params=_params("arbitrary"),
        name="conv",
    )(h, h, w, vec(b), vec(g), vec(beta))


def _merge_kernel(attn_ref, hc_ref, g_ref, x_ref, g1_ref, sh2_ref, sc2_ref, wa_ref, wc_ref,
                  wo_ref, lg_ref, lb_ref, wq_ref, x1_ref, u2_ref, qp_ref, *, alpha):
    d = x_ref.shape[1]
    ya = jnp.dot(attn_ref[...], wa_ref[...], preferred_element_type=F32)
    yb = jnp.dot(hc_ref[...], wc_ref[...], preferred_element_type=F32)
    merged = (g_ref[:, :d].astype(F32) * ya + g_ref[:, d:].astype(F32) * yb).astype(BF16)
    out = jnp.dot(merged, wo_ref[...], preferred_element_type=F32)
    x1 = _normalize(alpha * x_ref[...] + (1.0 + g1_ref[...]) * out) * lg_ref[...] + lb_ref[...]
    x1_ref[...] = x1
    u2 = (_normalize(x1) * (1.0 + sc2_ref[...]) + sh2_ref[...]).astype(BF16)
    u2_ref[...] = u2
    qp_ref[...] = jnp.dot(u2, wq_ref[...], preferred_element_type=F32).astype(qp_ref.dtype)


def _merge(attn, hc, gates, x2, g1, sh2, sc2, wa, wc, wo, lg, lb, wq, *, seq, alpha, tm):
    t, d = x2.shape
    tpb = seq // tm
    row = lambda i: (i, 0)
    per_batch = pl.BlockSpec((None, 1, d), lambda i: (i // tpb, 0, 0))
    nq = wq.shape[1]
    return pl.pallas_call(
        functools.partial(_merge_kernel, alpha=alpha),
        grid=(t // tm,),
        in_specs=[pl.BlockSpec((tm, attn.shape[1]), row), pl.BlockSpec((tm, hc.shape[1]), row),
                  pl.BlockSpec((tm, 2 * d), row), pl.BlockSpec((tm, d), row),
                  per_batch, per_batch, per_batch,
                  _resident(wa.shape), _resident(wc.shape), _resident(wo.shape),
                  _resident((1, d)), _resident((1, d)), _resident(wq.shape)],
        out_specs=[pl.BlockSpec((tm, d), row), pl.BlockSpec((tm, d), row),
                   pl.BlockSpec((tm, nq), row)],
        out_shape=[jax.ShapeDtypeStruct((t, d), F32), jax.ShapeDtypeStruct((t, d), BF16),
                   jax.ShapeDtypeStruct((t, nq), BF16)],
        compiler_params=_params("arbitrary"),
        name="merge",
    )(attn, hc, gates, x2, g1, sh2, sc2, wa, wc, wo, lg.reshape(1, d), lb.reshape(1, d), wq)


def _staircase(k):
    return [(a, b) for a in range(k) for b in range(k) if (a + 1) * (b + 1) <= k]


ID_PAD = 1e9


def _extract_topk(s, ids, k):
    vals, idxs = [], []
    for _ in range(k):
        m = jnp.max(s, axis=0, keepdims=True)
        i = jnp.min(jnp.where(s == m, ids, ID_PAD), axis=0, keepdims=True)
        s = jnp.where(ids == i, NEG_INF, s)
        vals.append(m)
        idxs.append(i)
    return jnp.concatenate(vals, axis=0), jnp.concatenate(idxs, axis=0)


RETRIEVE_HEADS = 2


def _retrieve_kernel(qp_ref, keys_ref, flat_ref, ii_ref, jj_ref, gg_ref, *, dhalf, nkeys, topk):
    tl = qp_ref.shape[0]
    qb = qp_ref[...]
    key_ids = lax.broadcasted_iota(I32, (nkeys, tl), 0).astype(F32)
    pairs = _staircase(topk)
    npad = flat_ref.shape[0]
    rank = lax.broadcasted_iota(I32, (topk, tl), 0)
    for hh in range(RETRIEVE_HEADS):
        sv, si = [], []
        for half in range(2):
            col = (2 * hh + half) * dhalf
            s = lax.dot_general(keys_ref[half], qb[:, col:col + dhalf],
                                (((1,), (1,)), ((), ())), preferred_element_type=F32)
            v, i = _extract_topk(s, key_ids, topk)
            sv.append(v)
            si.append(i)

        cand = [sv[0][a:a + 1, :] + sv[1][b:b + 1, :] for a, b in pairs]
        cand += [jnp.full((1, tl), NEG_INF, F32)] * (npad - len(pairs))
        cand = jnp.concatenate(cand, axis=0)
        cv, ci = _extract_topk(cand, flat_ref[...], topk)

        ci = ci.astype(I32)
        ia = lax.shift_right_logical(ci, int(math.log2(topk)))
        ib = ci & (topk - 1)
        ii_rows, jj_rows = [], []
        for r in range(topk):
            ii_rows.append(jnp.sum(jnp.where(rank == ia[r:r + 1, :], si[0], 0.0), axis=0,
                                   keepdims=True))
            jj_rows.append(jnp.sum(jnp.where(rank == ib[r:r + 1, :], si[1], 0.0), axis=0,
                                   keepdims=True))
        rows = slice(hh * topk, (hh + 1) * topk)
        ii_ref[rows, :] = jnp.concatenate(ii_rows, axis=0).astype(I32)
        jj_ref[rows, :] = jnp.concatenate(jj_rows, axis=0).astype(I32)
        e = jnp.exp(cv - cv[0:1, :])
        gg_ref[rows, :] = e / jnp.sum(e, axis=0, keepdims=True)


def _retrieve(qp, keys, *, tl):
    t, nq = qp.shape
    _, nkeys, dhalf = keys.shape
    heads = nq // (2 * dhalf)
    assert 2 * dhalf == LANES and PEER_TOPK & (PEER_TOPK - 1) == 0 and heads % RETRIEVE_HEADS == 0
    pairs = _staircase(PEER_TOPK)
    npad = -(-len(pairs) // 8) * 8
    flat = np.full((npad, tl), ID_PAD / 2, np.float32)
    flat[:len(pairs), :] = np.asarray([a * PEER_TOPK + b for a, b in pairs], np.float32)[:, None]
    kern = functools.partial(_retrieve_kernel, dhalf=dhalf, nkeys=nkeys, topk=PEER_TOPK)
    out = jax.ShapeDtypeStruct((heads * PEER_TOPK, t), I32)
    blk = pl.BlockSpec((RETRIEVE_HEADS * PEER_TOPK, tl), lambda i, h: (h, i))
    return pl.pallas_call(
        kern,
        grid=(t // tl, heads // RETRIEVE_HEADS),
        in_specs=[pl.BlockSpec((tl, RETRIEVE_HEADS * LANES), lambda i, h: (i, h)),
                  _resident(keys.shape),
                  _resident(flat.shape)],
        out_specs=[blk, blk, blk],
        out_shape=[out, out, jax.ShapeDtypeStruct((heads * PEER_TOPK, t), F32)],
        compiler_params=_params("arbitrary", "arbitrary"),
        name="retrieve",
    )(qp, keys, jnp.asarray(flat))


def _gelu_tanh(a):
    return 0.5 * a * (1.0 + jnp.tanh(math.sqrt(2.0 / math.pi) * (a + 0.044715 * a * a * a)))


def _peer_act_kernel(x_ref, u_ref, iit_ref, jjt_ref, ggt_ref, w_ref, ii_ref, jj_ref,
                     *, groups, nchunks):
    c = pl.program_id(1)

    @pl.when(c == 0)
    def _():
        ii_ref[...] = iit_ref[...].T
        jj_ref[...] = jjt_ref[...].T
        w_ref[...] = jnp.zeros(w_ref.shape, F32)

    a_all = lax.dot_general(x_ref[...], u_ref[...], (((1,), (1,)), ((), ())),
                            preferred_element_type=F32)
    ii = ii_ref[...]
    jj = jj_ref[...]
    acc = w_ref[...]
    for g in range(groups):
        picked = jnp.take_along_axis(a_all[:, g * LANES:(g + 1) * LANES], jj, axis=1)
        acc = jnp.where(ii == c * groups + g, picked, acc)
    w_ref[...] = acc

    @pl.when(c == nchunks - 1)
    def _():
        w_ref[...] = _gelu_tanh(w_ref[...]) * ggt_ref[...].T


def _peer_act(u2, utab, ii_t, jj_t, gg_t, *, tb, groups):
    t, d = u2.shape
    ne = utab.shape[0]
    hk = ii_t.shape[0]
    ce = groups * LANES
    nchunks = ne // ce
    tok = lambda i, c: (i, 0)
    hk_major = pl.BlockSpec((hk, tb), lambda i, c: (0, i))
    return pl.pallas_call(
        functools.partial(_peer_act_kernel, groups=groups, nchunks=nchunks),
        grid=(t // tb, nchunks),
        in_specs=[pl.BlockSpec((tb, d), tok), pl.BlockSpec((ce, d), lambda i, c: (c, 0)),
                  hk_major, hk_major, hk_major],
        out_specs=[pl.BlockSpec((tb, hk), tok)] * 3,
        out_shape=[jax.ShapeDtypeStruct((t, hk), F32), jax.ShapeDtypeStruct((t, hk), I32),
                   jax.ShapeDtypeStruct((t, hk), I32)],
        compiler_params=_params("arbitrary", "arbitrary"),
        name="peer_act",
    )(u2, utab, ii_t, jj_t, gg_t)


SCATTER_UNROLL = 32


def _w_pitch(tb):
    p = -(-tb // 8)
    return 8 * (p if p % 2 else p + 1)


def _peer_out_kernel(w_ref, ii_ref, jj_ref, v_ref, x1_ref, g2_ref, lg_ref, lb_ref,
                     o_ref, w_sc, acc_sc, *, tb, groups, nchunks, alpha):
    c = pl.program_id(1)
    pitch = _w_pitch(tb)

    @pl.when(c == 0)
    def _():
        acc_sc[...] = jnp.zeros(acc_sc.shape, F32)
        sub = lax.broadcasted_iota(I32, (LANES, LANES), 0)

        def scatter(tok, carry):
            wrow = w_ref[pl.ds(tok, 1), :]
            lhs = jnp.where(sub == ii_ref[pl.ds(tok, 1), :], wrow, 0.0).astype(BF16)
            rhs = jnp.where(sub == jj_ref[pl.ds(tok, 1), :], 1.0, 0.0).astype(BF16)
            w_sc[pl.ds(tok, LANES, stride=pitch), :] = lax.dot_general(
                lhs, rhs, (((1,), (1,)), ((), ())), preferred_element_type=F32)
            return carry

        lax.fori_loop(0, tb, scatter, 0, unroll=SCATTER_UNROLL)

    parts = [w_sc[pl.ds(pl.multiple_of((c * groups + g) * pitch, 8), tb), :].astype(BF16)
             for g in range(groups)]
    acc_sc[...] += jnp.dot(jnp.concatenate(parts, axis=1), v_ref[...],
                           preferred_element_type=F32)

    @pl.when(c == nchunks - 1)
    def _():
        r = alpha * x1_ref[...] + (1.0 + g2_ref[...]) * acc_sc[...]
        o_ref[...] = _normalize(r) * lg_ref[...] + lb_ref[...]


def _peer_out(w, ii, jj, vtab, x1, g2, lg, lb, *, seq, alpha, tb, groups):
    t, d = x1.shape
    ne = vtab.shape[0]
    assert ne == LANES * LANES
    ce = groups * LANES
    nchunks = ne // ce
    tpb = seq // tb
    tok = lambda i, c: (i, 0)
    kern = functools.partial(_peer_out_kernel, tb=tb, groups=groups, nchunks=nchunks, alpha=alpha)
    return pl.pallas_call(
        kern,
        grid=(t // tb, nchunks),
        in_specs=[pl.BlockSpec((tb, LANES), tok), pl.BlockSpec((tb, LANES), tok),
                  pl.BlockSpec((tb, LANES), tok),
                  pl.BlockSpec((ce, d), lambda i, c: (c, 0)),
                  pl.BlockSpec((tb, d), tok, pipeline_mode=pl.Buffered(1)),
                  pl.BlockSpec((None, 1, d), lambda i, c: (i // tpb, 0, 0)),
                  _resident((1, d)), _resident((1, d))],
        out_specs=pl.BlockSpec((tb, d), tok),
        out_shape=jax.ShapeDtypeStruct((t, d), F32),
        scratch_shapes=[pltpu.VMEM((LANES * _w_pitch(tb), LANES), F32), pltpu.VMEM((tb, d), F32)],
        compiler_params=_params("arbitrary", "arbitrary"),
        name="peer_out",
    )(w, ii, jj, vtab, x1, g2, lg.reshape(1, d), lb.reshape(1, d))


def _tile(n, want):
    t = min(n, want)
    while n % t:
        t -= 1
    return t


def kernel(x, c, w_mod, b_mod, w_in, b_f, conv_w, conv_b, conv_ln_g, conv_ln_b,
           w_attn_out, w_conv_out, w_out, ln1_g, ln1_b, peer_wq, peer_sub_keys,
           peer_u, peer_v, ln2_g, ln2_b):
    bsz, seq, d = x.shape
    depth = w_mod.shape[0]
    t = bsz * seq
    att_w = w_attn_out.shape[1]
    n_heads = b_f.shape[1]
    head_dim = att_w // n_heads
    conv_c = conv_w.shape[2]
    alpha = (2.0 * depth) ** 0.25
    assert n_heads <= LANES and LANES % head_dim == 0 and att_w % LANES == 0

    tm = _tile(seq, 512)
    ta = _tile(seq, 2048)
    tp = _tile(seq, 1024)
    xf = x.reshape(t, d)
    for l in range(depth):
        mod = _mod(c, w_mod[l], b_mod[l])
        sh1, sc1, g1, sh2, sc2, g2 = [m.reshape(bsz, 1, d) for m in jnp.split(mod, 6, axis=-1)]

        o_f = 3 * att_w
        o_c = o_f + n_heads
        o_g = o_c + 2 * conv_c
        wl = w_in[l]
        wqkv = wl[:, :o_f].astype(BF16)
        wf = jnp.pad(wl[:, o_f:o_c], ((0, 0), (0, LANES - n_heads))).astype(BF16)
        wc = wl[:, o_c:o_g].astype(BF16)
        wg = wl[:, o_g:].astype(BF16)
        qkv, f, h, gates = _inproj(xf, sh1, sc1, wqkv, wf, wc, wg, seq=seq, att_w=att_w,
                                   conv_c=conv_c, head_dim=head_dim, tm=tm)

        bf_row = jnp.pad(b_f[l], (0, LANES - n_heads)).reshape(1, LANES)
        qb, kb = _fcum(f.reshape(bsz, seq, LANES), bf_row, n_heads=n_heads, tc=_tile(seq, 256))
        attn = _attention(qkv, qb.reshape(t, -1), kb.reshape(t, -1), bsz=bsz, seq=seq,
                          att_w=att_w, head_dim=head_dim, tq=ta, tk=ta, rb=_tile(ta, 512))

        hc = _conv(h, conv_w[l], conv_b[l], conv_ln_g[l], conv_ln_b[l], seq=seq, ts=tm)

        x1, u2, qp = _merge(attn, hc, gates, xf, g1, sh2, sc2,
                            w_attn_out[l].astype(BF16), w_conv_out[l].astype(BF16),
                            w_out[l].astype(BF16), ln1_g[l], ln1_b[l],
                            peer_wq[l].astype(BF16), seq=seq, alpha=alpha, tm=tm)

        ii_t, jj_t, gg_t = _retrieve(qp, peer_sub_keys[l].astype(BF16), tl=tm)
        w, ii, jj = _peer_act(u2, peer_u[l].astype(BF16), ii_t, jj_t, gg_t, tb=tp, groups=32)
        xf = _peer_out(w, ii, jj, peer_v[l].astype(BF16), x1, g2, ln2_g[l], ln2_b[l],
                       seq=seq, alpha=alpha, tb=tm, groups=16)
    return xf.reshape(bsz, seq, d)
```

```python
import functools
import math

import jax
import jax.numpy as jnp
import numpy as np
from jax import lax
from jax.experimental import pallas as pl
from jax.experimental.pallas import tpu as pltpu

F32 = jnp.float32
BF16 = jnp.bfloat16
I32 = jnp.int32

LN_EPS = 1e-5
PEER_TOPK = 16
LANES = 128
VMEM_LIMIT_BYTES = 56 * 1024 * 1024
NEG_INF = float("-inf")
LOG2_E = math.log2(math.e)


def _params(*semantics):
    return pltpu.CompilerParams(dimension_semantics=semantics, vmem_limit_bytes=VMEM_LIMIT_BYTES)


def _normalize(x):
    mu = jnp.mean(x, axis=-1, keepdims=True)
    xc = x - mu
    var = jnp.mean(xc * xc, axis=-1, keepdims=True)
    return xc * lax.rsqrt(var + LN_EPS)


def _resident(shape):
    nd = len(shape)
    return pl.BlockSpec(shape, lambda *_: (0,) * nd)


def _mod_kernel(c_ref, w_ref, b_ref, o_ref):
    c = c_ref[...]
    cs = c * jax.nn.sigmoid(c)
    o_ref[...] = jnp.dot(cs, w_ref[...], preferred_element_type=F32,
                         precision=lax.Precision.HIGHEST) + b_ref[...]


def _mod(c, w, b):
    bsz, d = c.shape
    n = w.shape[1]
    rows = -(-bsz // 8) * 8
    tn = math.gcd(n, 1536)
    cp = jnp.pad(c, ((0, rows - bsz), (0, 0)))
    out = pl.pallas_call(
        _mod_kernel,
        grid=(n // tn,),
        in_specs=[pl.BlockSpec((rows, d), lambda j: (0, 0)),
                  pl.BlockSpec((d, tn), lambda j: (0, j)),
                  pl.BlockSpec((1, tn), lambda j: (0, j))],
        out_specs=pl.BlockSpec((rows, tn), lambda j: (0, j)),
        out_shape=jax.ShapeDtypeStruct((rows, n), F32),
        compiler_params=_params("arbitrary"),
        name="mod",
    )(cp, w, b.reshape(1, n))
    return out[:bsz]


def _inproj_kernel(x_ref, sh_ref, sc_ref, wqkv_ref, wf_ref, wc_ref, wg_ref,
                   qkv_ref, f_ref, h_ref, g_ref, *, att_w, conv_c, qscale):
    y = _normalize(x_ref[...])
    u = (y * (1.0 + sc_ref[...]) + sh_ref[...]).astype(BF16)
    qkv = jnp.dot(u, wqkv_ref[...], preferred_element_type=F32)
    qkv_ref[:, :att_w] = (qkv[:, :att_w] * qscale).astype(BF16)
    qkv_ref[:, att_w:] = qkv[:, att_w:].astype(BF16)
    f_ref[...] = jnp.dot(u, wf_ref[...], preferred_element_type=F32)
    cab = jnp.dot(u, wc_ref[...], preferred_element_type=F32)
    h_ref[...] = cab[:, :conv_c] * jax.nn.sigmoid(cab[:, conv_c:])
    g_ref[...] = jax.nn.sigmoid(jnp.dot(u, wg_ref[...], preferred_element_type=F32))


def _inproj(x2, sh, sc, wqkv, wf, wc, wg, *, seq, att_w, conv_c, head_dim, tm):
    t, d = x2.shape
    tpb = seq // tm
    row = lambda i: (i, 0)
    per_batch = pl.BlockSpec((None, 1, d), lambda i: (i // tpb, 0, 0))
    kern = functools.partial(_inproj_kernel, att_w=att_w, conv_c=conv_c,
                             qscale=LOG2_E / math.sqrt(head_dim))
    return pl.pallas_call(
        kern,
        grid=(t // tm,),
        in_specs=[pl.BlockSpec((tm, d), row), per_batch, per_batch,
                  _resident(wqkv.shape), _resident(wf.shape), _resident(wc.shape),
                  _resident(wg.shape)],
        out_specs=[pl.BlockSpec((tm, 3 * att_w), row), pl.BlockSpec((tm, LANES), row),
                   pl.BlockSpec((tm, conv_c), row), pl.BlockSpec((tm, 2 * d), row)],
        out_shape=[jax.ShapeDtypeStruct((t, 3 * att_w), BF16),
                   jax.ShapeDtypeStruct((t, LANES), F32),
                   jax.ShapeDtypeStruct((t, conv_c), F32),
                   jax.ShapeDtypeStruct((t, 2 * d), F32)],
        compiler_params=_params("arbitrary"),
        name="inproj",
    )(x2, sh, sc, wqkv, wf, wc, wg)


BIAS_TERMS = 3


def _fcum_kernel(f_ref, bf_ref, qb_ref, kb_ref, carry_ref, *, tc, n_heads):
    @pl.when(pl.program_id(1) == 0)
    def _():
        carry_ref[...] = jnp.zeros(carry_ref.shape, F32)

    z = f_ref[...] + bf_ref[...]
    logf = jnp.minimum(z, 0.0) - jnp.log1p(jnp.exp(-jnp.abs(z)))
    r = lax.broadcasted_iota(I32, (tc, tc), 0)
    c = lax.broadcasted_iota(I32, (tc, tc), 1)
    tri = jnp.where(c <= r, 1.0, 0.0).astype(F32)
    cs = jnp.dot(tri, logf, preferred_element_type=F32,
                 precision=lax.Precision.HIGHEST) + carry_ref[...]
    carry_ref[...] = cs[tc - 1:tc, :]
    rem = cs * LOG2_E
    terms = []
    for _ in range(BIAS_TERMS):
        term = rem.astype(BF16)
        terms.append(term.astype(F32))
        rem = rem - terms[-1]
    lane = lax.broadcasted_iota(I32, (tc, LANES), 1)
    for hh in range(n_heads):
        qb = jnp.where((lane >= BIAS_TERMS) & (lane < 2 * BIAS_TERMS), -1.0, 0.0)
        kb = jnp.where(lane < BIAS_TERMS, 1.0, 0.0)
        for r, term in enumerate(terms):
            col = jnp.broadcast_to(term[:, hh:hh + 1], (tc, LANES))
            qb = jnp.where(lane == r, col, qb)
            kb = jnp.where(lane == BIAS_TERMS + r, col, kb)
        qb_ref[:, hh * LANES:(hh + 1) * LANES] = qb.astype(BF16)
        kb_ref[:, hh * LANES:(hh + 1) * LANES] = kb.astype(BF16)


def _fcum(f3, bf_row, *, n_heads, tc):
    bsz, seq, w = f3.shape
    slab = pl.BlockSpec((None, tc, n_heads * LANES), lambda b, j: (b, j, 0))
    out = jax.ShapeDtypeStruct((bsz, seq, n_heads * LANES), BF16)
    return pl.pallas_call(
        functools.partial(_fcum_kernel, tc=tc, n_heads=n_heads),
        grid=(bsz, seq // tc),
        in_specs=[pl.BlockSpec((None, tc, w), lambda b, j: (b, j, 0)),
                  pl.BlockSpec((1, w), lambda b, j: (0, 0))],
        out_specs=[slab, slab],
        out_shape=[out, out],
        scratch_shapes=[pltpu.VMEM((1, w), F32)],
        compiler_params=_params("arbitrary", "arbitrary"),
        name="fcum",
    )(f3, bf_row)


ATTN_MASKED = 1
ATTN_LAST = 2


def _attn_tables(nq, tq, tk):
    qi_l, ki_l, fl_l = [], [], []
    for qi in range(nq):
        last = (qi * tq + tq - 1) // tk
        for ki in range(last + 1):
            masked = ki * tk + tk - 1 > qi * tq
            qi_l.append(qi)
            ki_l.append(ki)
            fl_l.append((ATTN_MASKED if masked else 0) | (ATTN_LAST if ki == last else 0))
    as_i32 = lambda v: jnp.asarray(np.asarray(v, np.int32))
    return as_i32(qi_l), as_i32(ki_l), as_i32(fl_l)


def _attn_kernel(qi_tab, ki_tab, fl_tab, q_ref, qb_ref, k_ref, kb_ref, v_ref, o_ref,
                 qop_sc, m_sc, l_sc, acc_sc, *, tq, tk, rb, dh, heads):
    p = pl.program_id(2)
    qi = qi_tab[p]
    ki = ki_tab[p]
    fl = fl_tab[p]
    head_of_lane = lax.broadcasted_iota(I32, (tq, LANES), 1) // dh

    @pl.when(ki == 0)
    def _():
        m_sc[...] = jnp.full(m_sc.shape, NEG_INF, F32)
        l_sc[...] = jnp.zeros(l_sc.shape, F32)
        acc_sc[...] = jnp.zeros(acc_sc.shape, F32)
        q = q_ref[...]
        for hh in range(heads):
            qop_sc[hh, :, :LANES] = jnp.where(head_of_lane == hh, q, jnp.zeros_like(q))
            qop_sc[hh, :, LANES:] = qb_ref[:, hh * LANES:(hh + 1) * LANES]

    def per_head(vals):
        out = vals[0]
        lane_head = lax.broadcasted_iota(I32, out.shape, 1) // dh
        for hh in range(1, heads):
            out = jnp.where(lane_head == hh, vals[hh], out)
        return out

    def step(diagonal):
        for r0 in range(0, tq, rb):
            rows = slice(r0, r0 + rb)
            nk = min(tk, r0 + rb) if diagonal else tk
            if diagonal:
                keep = (lax.broadcasted_iota(I32, (rb, nk), 1)
                        <= r0 + lax.broadcasted_iota(I32, (rb, nk), 0))
            alphas, pvs = [], []
            for hh in range(heads):
                kop = jnp.concatenate([k_ref[:nk, :], kb_ref[:nk, hh * LANES:(hh + 1) * LANES]],
                                      axis=1)
                s = lax.dot_general(qop_sc[hh, rows, :], kop, (((1,), (1,)), ((), ())),
                                    preferred_element_type=F32)
                if diagonal:
                    s = jnp.where(keep, s, NEG_INF)
                m_prev = m_sc[hh, rows, :]
                m_new = jnp.maximum(m_prev, jnp.max(s, axis=-1, keepdims=True))
                alpha = jnp.exp2(m_prev - m_new)
                pr = jnp.exp2(s - jnp.concatenate([m_new] * (nk // LANES), axis=1))
                l_sc[hh, rows, :] = alpha * l_sc[hh, rows, :] + jnp.sum(pr, axis=-1, keepdims=True)
                m_sc[hh, rows, :] = m_new
                alphas.append(alpha)
                pvs.append(jnp.dot(pr.astype(BF16), v_ref[:nk, :], preferred_element_type=F32))
            acc_sc[rows, :] = per_head(alphas) * acc_sc[rows, :] + per_head(pvs)

    pl.when((fl & ATTN_MASKED) != 0)(lambda: step(True))
    pl.when((fl & ATTN_MASKED) == 0)(lambda: step(False))

    @pl.when((fl & ATTN_LAST) != 0)
    def _():
        o_ref[...] = (acc_sc[...] / per_head([l_sc[hh] for hh in range(heads)])).astype(o_ref.dtype)


def _attention(qkv, qb, kb, *, bsz, seq, att_w, head_dim, tq, tk, rb):
    heads = LANES // head_dim
    ncol = att_w // LANES
    nq, nk = seq // tq, seq // tk
    assert tq == tk and tq % rb == 0 and rb % LANES == 0
    qi_tab, ki_tab, fl_tab = _attn_tables(nq, tq, tk)
    kern = functools.partial(_attn_kernel, tq=tq, tk=tk, rb=rb, dh=head_dim, heads=heads)
    qrow = lambda b, c, p, qt, kt, ft: b * nq + qt[p]
    krow = lambda b, c, p, qt, kt, ft: b * nk + kt[p]
    grid_spec = pltpu.PrefetchScalarGridSpec(
        num_scalar_prefetch=3,
        grid=(bsz, ncol, int(qi_tab.shape[0])),
        in_specs=[
            pl.BlockSpec((tq, LANES), lambda b, c, *a: (qrow(b, c, *a), c)),
            pl.BlockSpec((tq, heads * LANES), lambda b, c, *a: (qrow(b, c, *a), c)),
            pl.BlockSpec((tk, LANES), lambda b, c, *a: (krow(b, c, *a), ncol + c)),
            pl.BlockSpec((tk, heads * LANES), lambda b, c, *a: (krow(b, c, *a), c)),
            pl.BlockSpec((tk, LANES), lambda b, c, *a: (krow(b, c, *a), 2 * ncol + c)),
        ],
        out_specs=pl.BlockSpec((tq, LANES), lambda b, c, *a: (qrow(b, c, *a), c)),
        scratch_shapes=[pltpu.VMEM((heads, tq, 2 * LANES), BF16),
                        pltpu.VMEM((heads, tq, LANES), F32), pltpu.VMEM((heads, tq, LANES), F32),
                        pltpu.VMEM((tq, LANES), F32)],
    )
    return pl.pallas_call(
        kern,
        grid_spec=grid_spec,
        out_shape=jax.ShapeDtypeStruct((bsz * seq, att_w), BF16),
        compiler_params=_params("arbitrary", "arbitrary", "arbitrary"),
        name="attn",
    )(qi_tab, ki_tab, fl_tab, qkv, qb, qkv, kb, qkv)


CONV_HALO = 32
CONV_CHUNK = 64


SUBLANES = 8


def _conv_kernel(h_ref, halo_ref, w_ref, b_ref, g_ref, beta_ref, o_ref, sh_sc, *, ts, taps, tpb):
    first = pl.program_id(0) % tpb == 0
    rows = ts + CONV_HALO - SUBLANES
    sh_sc[0, 0:CONV_HALO, :] = jnp.where(first, 0.0, halo_ref[...])
    sh_sc[0, CONV_HALO:, :] = h_ref[...]
    for r in range(1, SUBLANES):
        sh_sc[r, 0:rows, :] = sh_sc[0, r:r + rows, :]
    lead = CONV_HALO - (taps - 1)
    for c0 in range(0, ts, CONV_CHUNK):
        acc = jnp.broadcast_to(b_ref[...], (CONV_CHUNK, b_ref.shape[1]))
        for k in range(taps):
            r = (lead + k) % SUBLANES
            start = c0 + lead + k - r
            acc = acc + w_ref[k:k + 1, :] * sh_sc[r, start:start + CONV_CHUNK, :]
        y = _normalize(acc) * g_ref[...] + beta_ref[...]
        o_ref[c0:c0 + CONV_CHUNK, :] = (y * jax.nn.sigmoid(y)).astype(o_ref.dtype)


def _conv(h, w, b, g, beta, *, seq, ts):
    t, c = h.shape
    taps = w.shape[0]
    assert taps - 1 <= CONV_HALO and ts % CONV_CHUNK == 0 and ts % CONV_HALO == 0
    tpb = seq // ts
    hb = ts // CONV_HALO
    vec = lambda v: v.reshape(1, c)
    return pl.pallas_call(
        functools.partial(_conv_kernel, ts=ts, taps=taps, tpb=tpb),
        grid=(t // ts,),
        in_specs=[pl.BlockSpec((ts, c), lambda i: (i, 0)),
                  pl.BlockSpec((CONV_HALO, c), lambda i: (jnp.maximum(i * hb - 1, 0), 0)),
                  _resident((taps, c)), _resident((1, c)), _resident((1, c)), _resident((1, c))],
        out_specs=pl.BlockSpec((ts, c), lambda i: (i, 0)),
        out_shape=jax.ShapeDtypeStruct((t, c), BF16),
        scratch_shapes=[pltpu.VMEM((SUBLANES, ts + CONV_HALO, c), F32)],
        compiler_params=_params("arbitrary"),
        name="conv",
    )(h, h, w, vec(b), vec(g), vec(beta))


def _merge_kernel(attn_ref, hc_ref, g_ref, x_ref, g1_ref, sh2_ref, sc2_ref, wa_ref, wc_ref,
                  wo_ref, lg_ref, lb_ref, wq_ref, x1_ref, u2_ref, qp_ref, *, alpha):
    d = x_ref.shape[1]
    ya = jnp.dot(attn_ref[...], wa_ref[...], preferred_element_type=F32)
    yb = jnp.dot(hc_ref[...], wc_ref[...], preferred_element_type=F32)
    merged = (g_ref[:, :d] * ya + g_ref[:, d:] * yb).astype(BF16)
    out = jnp.dot(merged, wo_ref[...], preferred_element_type=F32)
    x1 = _normalize(alpha * x_ref[...] + (1.0 + g1_ref[...]) * out) * lg_ref[...] + lb_ref[...]
    x1_ref[...] = x1
    u2 = (_normalize(x1) * (1.0 + sc2_ref[...]) + sh2_ref[...]).astype(BF16)
    u2_ref[...] = u2
    qp_ref[...] = jnp.dot(u2, wq_ref[...], preferred_element_type=F32).astype(qp_ref.dtype)


def _merge(attn, hc, gates, x2, g1, sh2, sc2, wa, wc, wo, lg, lb, wq, *, seq, alpha, tm):
    t, d = x2.shape
    tpb = seq // tm
    row = lambda i: (i, 0)
    per_batch = pl.BlockSpec((None, 1, d), lambda i: (i // tpb, 0, 0))
    nq = wq.shape[1]
    return pl.pallas_call(
        functools.partial(_merge_kernel, alpha=alpha),
        grid=(t // tm,),
        in_specs=[pl.BlockSpec((tm, attn.shape[1]), row), pl.BlockSpec((tm, hc.shape[1]), row),
                  pl.BlockSpec((tm, 2 * d), row), pl.BlockSpec((tm, d), row),
                  per_batch, per_batch, per_batch,
                  _resident(wa.shape), _resident(wc.shape), _resident(wo.shape),
                  _resident((1, d)), _resident((1, d)), _resident(wq.shape)],
        out_specs=[pl.BlockSpec((tm, d), row), pl.BlockSpec((tm, d), row),
                   pl.BlockSpec((tm, nq), row)],
        out_shape=[jax.ShapeDtypeStruct((t, d), F32), jax.ShapeDtypeStruct((t, d), BF16),
                   jax.ShapeDtypeStruct((t, nq), BF16)],
        compiler_params=_params("arbitrary"),
        name="merge",
    )(attn, hc, gates, x2, g1, sh2, sc2, wa, wc, wo, lg.reshape(1, d), lb.reshape(1, d), wq)


def _staircase(k):
    return [(a, b) for a in range(k) for b in range(k) if (a + 1) * (b + 1) <= k]


ID_PAD = 1e9


def _extract_topk(s, ids, k):
    vals, idxs = [], []
    for _ in range(k):
        m = jnp.max(s, axis=0, keepdims=True)
        i = jnp.min(jnp.where(s == m, ids, ID_PAD), axis=0, keepdims=True)
        s = jnp.where(ids == i, NEG_INF, s)
        vals.append(m)
        idxs.append(i)
    return jnp.concatenate(vals, axis=0), jnp.concatenate(idxs, axis=0)


def _topk_columns(s, k):
    n, tl = s.shape
    ng = n // SUBLANES
    vals = [s[SUBLANES * g:SUBLANES * (g + 1), :] for g in range(ng)]
    sub = lax.broadcasted_iota(I32, (SUBLANES, tl), 0).astype(F32)
    ids = [sub + float(SUBLANES * g) for g in range(ng)]
    for phase in range(ng):
        for a in range(phase % 2, ng - 1, 2):
            up = vals[a + 1] > vals[a]
            vals[a], vals[a + 1] = (jnp.maximum(vals[a], vals[a + 1]),
                                    jnp.minimum(vals[a], vals[a + 1]))
            ids[a], ids[a + 1] = (jnp.where(up, ids[a + 1], ids[a]),
                                  jnp.where(up, ids[a], ids[a + 1]))
    out_v, out_i = [], []
    for r in range(k):
        m = jnp.max(vals[0], axis=0, keepdims=True)
        i = jnp.min(jnp.where(vals[0] == m, ids[0], ID_PAD), axis=0, keepdims=True)
        out_v.append(m)
        out_i.append(i)
        win = ids[0] == i
        live = min(ng, k - r)
        for g in range(live - 1):
            vals[g] = jnp.where(win, vals[g + 1], vals[g])
            ids[g] = jnp.where(win, ids[g + 1], ids[g])
        if live == ng:
            vals[ng - 1] = jnp.where(win, NEG_INF, vals[ng - 1])
    return jnp.concatenate(out_v, axis=0), jnp.concatenate(out_i, axis=0)


RETRIEVE_HEADS = 2


def _retrieve_kernel(qp_ref, keys_ref, flat_ref, ii_ref, jj_ref, gg_ref, *, dhalf, nkeys, topk):
    tl = qp_ref.shape[0]
    qb = qp_ref[...]
    pairs = _staircase(topk)
    npad = flat_ref.shape[0]
    rank = lax.broadcasted_iota(I32, (topk, tl), 0)
    for hh in range(RETRIEVE_HEADS):
        sv, si = [], []
        for half in range(2):
            col = (2 * hh + half) * dhalf
            s = lax.dot_general(keys_ref[half], qb[:, col:col + dhalf],
                                (((1,), (1,)), ((), ())), preferred_element_type=F32)
            v, i = _topk_columns(s, topk)
            sv.append(v)
            si.append(i)

        cand = [sv[0][a:a + 1, :] + sv[1][b:b + 1, :] for a, b in pairs]
        cand += [jnp.full((1, tl), NEG_INF, F32)] * (npad - len(pairs))
        cand = jnp.concatenate(cand, axis=0)
        cv, ci = _extract_topk(cand, flat_ref[...], topk)

        ci = ci.astype(I32)
        ia = lax.shift_right_logical(ci, int(math.log2(topk)))
        ib = ci & (topk - 1)
        ii_rows, jj_rows = [], []
        for r in range(topk):
            ii_rows.append(jnp.sum(jnp.where(rank == ia[r:r + 1, :], si[0], 0.0), axis=0,
                                   keepdims=True))
            jj_rows.append(jnp.sum(jnp.where(rank == ib[r:r + 1, :], si[1], 0.0), axis=0,
                                   keepdims=True))
        rows = slice(hh * topk, (hh + 1) * topk)
        ii_ref[rows, :] = jnp.concatenate(ii_rows, axis=0).astype(I32)
        jj_ref[rows, :] = jnp.concatenate(jj_rows, axis=0).astype(I32)
        e = jnp.exp(cv - cv[0:1, :])
        gg_ref[rows, :] = e / jnp.sum(e, axis=0, keepdims=True)


def _retrieve(qp, keys, *, tl):
    t, nq = qp.shape
    _, nkeys, dhalf = keys.shape
    heads = nq // (2 * dhalf)
    assert 2 * dhalf == LANES and PEER_TOPK & (PEER_TOPK - 1) == 0 and heads % RETRIEVE_HEADS == 0
    pairs = _staircase(PEER_TOPK)
    npad = -(-len(pairs) // 8) * 8
    flat = np.full((npad, tl), ID_PAD / 2, np.float32)
    flat[:len(pairs), :] = np.asarray([a * PEER_TOPK + b for a, b in pairs], np.float32)[:, None]
    kern = functools.partial(_retrieve_kernel, dhalf=dhalf, nkeys=nkeys, topk=PEER_TOPK)
    out = jax.ShapeDtypeStruct((heads * PEER_TOPK, t), I32)
    blk = pl.BlockSpec((RETRIEVE_HEADS * PEER_TOPK, tl), lambda i, h: (h, i))
    return pl.pallas_call(
        kern,
        grid=(t // tl, heads // RETRIEVE_HEADS),
        in_specs=[pl.BlockSpec((tl, RETRIEVE_HEADS * LANES), lambda i, h: (i, h)),
                  _resident(keys.shape),
                  _resident(flat.shape)],
        out_specs=[blk, blk, blk],
        out_shape=[out, out, jax.ShapeDtypeStruct((heads * PEER_TOPK, t), F32)],
        compiler_params=_params("arbitrary", "arbitrary"),
        name="retrieve",
    )(qp, keys, jnp.asarray(flat))


def _gelu_tanh(a):
    return 0.5 * a * (1.0 + jnp.tanh(math.sqrt(2.0 / math.pi) * (a + 0.044715 * a * a * a)))


def _peer_act_kernel(x_ref, u_ref, iit_ref, jjt_ref, ggt_ref, w_ref, ii_ref, jj_ref,
                     *, groups, nchunks):
    c = pl.program_id(1)

    @pl.when(c == 0)
    def _():
        ii_ref[...] = iit_ref[...].T
        jj_ref[...] = jjt_ref[...].T
        w_ref[...] = jnp.zeros(w_ref.shape, F32)

    a_all = lax.dot_general(x_ref[...], u_ref[...], (((1,), (1,)), ((), ())),
                            preferred_element_type=F32)
    ii = ii_ref[...]
    jj = jj_ref[...]
    acc = w_ref[...]
    for g in range(groups):
        picked = jnp.take_along_axis(a_all[:, g * LANES:(g + 1) * LANES], jj, axis=1)
        acc = jnp.where(ii == c * groups + g, picked, acc)
    w_ref[...] = acc

    @pl.when(c == nchunks - 1)
    def _():
        w_ref[...] = _gelu_tanh(w_ref[...]) * ggt_ref[...].T


def _peer_act(u2, utab, ii_t, jj_t, gg_t, *, tb, groups):
    t, d = u2.shape
    ne = utab.shape[0]
    hk = ii_t.shape[0]
    ce = groups * LANES
    nchunks = ne // ce
    tok = lambda i, c: (i, 0)
    hk_major = pl.BlockSpec((hk, tb), lambda i, c: (0, i))
    return pl.pallas_call(
        functools.partial(_peer_act_kernel, groups=groups, nchunks=nchunks),
        grid=(t // tb, nchunks),
        in_specs=[pl.BlockSpec((tb, d), tok), pl.BlockSpec((ce, d), lambda i, c: (c, 0)),
                  hk_major, hk_major, hk_major],
        out_specs=[pl.BlockSpec((tb, hk), tok)] * 3,
        out_shape=[jax.ShapeDtypeStruct((t, hk), F32), jax.ShapeDtypeStruct((t, hk), I32),
                   jax.ShapeDtypeStruct((t, hk), I32)],
        compiler_params=_params("arbitrary", "arbitrary"),
        name="peer_act",
    )(u2, utab, ii_t, jj_t, gg_t)


SCATTER_UNROLL = 32


def _w_pitch(tb):
    p = -(-tb // 8)
    return 8 * (p if p % 2 else p + 1)


def _peer_out_kernel(w_ref, ii_ref, jj_ref, v_ref, x1_ref, g2_ref, lg_ref, lb_ref,
                     o_ref, w_sc, acc_sc, *, tb, groups, nchunks, alpha):
    c = pl.program_id(1)
    pitch = _w_pitch(tb)

    @pl.when(c == 0)
    def _():
        acc_sc[...] = jnp.zeros(acc_sc.shape, F32)
        sub = lax.broadcasted_iota(I32, (LANES, LANES), 0)

        def scatter(tok, carry):
            wrow = w_ref[pl.ds(tok, 1), :]
            lhs = jnp.where(sub == ii_ref[pl.ds(tok, 1), :], wrow, 0.0).astype(BF16)
            rhs = jnp.where(sub == jj_ref[pl.ds(tok, 1), :], 1.0, 0.0).astype(BF16)
            w_sc[pl.ds(tok, LANES, stride=pitch), :] = lax.dot_general(
                lhs, rhs, (((1,), (1,)), ((), ())), preferred_element_type=F32)
            return carry

        lax.fori_loop(0, tb, scatter, 0, unroll=SCATTER_UNROLL)

    parts = [w_sc[pl.ds(pl.multiple_of((c * groups + g) * pitch, 8), tb), :].astype(BF16)
             for g in range(groups)]
    acc_sc[...] += jnp.dot(jnp.concatenate(parts, axis=1), v_ref[...],
                           preferred_element_type=F32)

    @pl.when(c == nchunks - 1)
    def _():
        r = alpha * x1_ref[...] + (1.0 + g2_ref[...]) * acc_sc[...]
        o_ref[...] = _normalize(r) * lg_ref[...] + lb_ref[...]


def _peer_out(w, ii, jj, vtab, x1, g2, lg, lb, *, seq, alpha, tb, groups):
    t, d = x1.shape
    ne = vtab.shape[0]
    assert ne == LANES * LANES
    ce = groups * LANES
    nchunks = ne // ce
    tpb = seq // tb
    tok = lambda i, c: (i, 0)
    kern = functools.partial(_peer_out_kernel, tb=tb, groups=groups, nchunks=nchunks, alpha=alpha)
    return pl.pallas_call(
        kern,
        grid=(t // tb, nchunks),
        in_specs=[pl.BlockSpec((tb, LANES), tok), pl.BlockSpec((tb, LANES), tok),
                  pl.BlockSpec((tb, LANES), tok),
                  pl.BlockSpec((ce, d), lambda i, c: (c, 0)),
                  pl.BlockSpec((tb, d), tok, pipeline_mode=pl.Buffered(1)),
                  pl.BlockSpec((None, 1, d), lambda i, c: (i // tpb, 0, 0)),
                  _resident((1, d)), _resident((1, d))],
        out_specs=pl.BlockSpec((tb, d), tok),
        out_shape=jax.ShapeDtypeStruct((t, d), F32),
        scratch_shapes=[pltpu.VMEM((LANES * _w_pitch(tb), LANES), F32), pltpu.VMEM((tb, d), F32)],
        compiler_params=_params("arbitrary", "arbitrary"),
        name="peer_out",
    )(w, ii, jj, vtab, x1, g2, lg.reshape(1, d), lb.reshape(1, d))


def _tile(n, want):
    t = min(n, want)
    while n % t:
        t -= 1
    return t


def kernel(x, c, w_mod, b_mod, w_in, b_f, conv_w, conv_b, conv_ln_g, conv_ln_b,
           w_attn_out, w_conv_out, w_out, ln1_g, ln1_b, peer_wq, peer_sub_keys,
           peer_u, peer_v, ln2_g, ln2_b):
    bsz, seq, d = x.shape
    depth = w_mod.shape[0]
    t = bsz * seq
    att_w = w_attn_out.shape[1]
    n_heads = b_f.shape[1]
    head_dim = att_w // n_heads
    conv_c = conv_w.shape[2]
    alpha = (2.0 * depth) ** 0.25
    assert n_heads <= LANES and LANES % head_dim == 0 and att_w % LANES == 0

    tm = _tile(seq, 512)
    ta = _tile(seq, 2048)
    tp = _tile(seq, 1024)
    xf = x.reshape(t, d)
    for l in range(depth):
        mod = _mod(c, w_mod[l], b_mod[l])
        sh1, sc1, g1, sh2, sc2, g2 = [m.reshape(bsz, 1, d) for m in jnp.split(mod, 6, axis=-1)]

        o_f = 3 * att_w
        o_c = o_f + n_heads
        o_g = o_c + 2 * conv_c
        wl = w_in[l]
        wqkv = wl[:, :o_f].astype(BF16)
        wf = jnp.pad(wl[:, o_f:o_c], ((0, 0), (0, LANES - n_heads))).astype(BF16)
        wc = wl[:, o_c:o_g].astype(BF16)
        wg = wl[:, o_g:].astype(BF16)
        qkv, f, h, gates = _inproj(xf, sh1, sc1, wqkv, wf, wc, wg, seq=seq, att_w=att_w,
                                   conv_c=conv_c, head_dim=head_dim, tm=tm)

        bf_row = jnp.pad(b_f[l], (0, LANES - n_heads)).reshape(1, LANES)
        qb, kb = _fcum(f.reshape(bsz, seq, LANES), bf_row, n_heads=n_heads, tc=_tile(seq, 256))
        attn = _attention(qkv, qb.reshape(t, -1), kb.reshape(t, -1), bsz=bsz, seq=seq,
                          att_w=att_w, head_dim=head_dim, tq=ta, tk=ta, rb=_tile(ta, 512))

        hc = _conv(h, conv_w[l], conv_b[l], conv_ln_g[l], conv_ln_b[l], seq=seq, ts=tm)

        x1, u2, qp = _merge(attn, hc, gates, xf, g1, sh2, sc2,
                            w_attn_out[l].astype(BF16), w_conv_out[l].astype(BF16),
                            w_out[l].astype(BF16), ln1_g[l], ln1_b[l],
                            peer_wq[l].astype(BF16), seq=seq, alpha=alpha, tm=tm)

        ii_t, jj_t, gg_t = _retrieve(qp, peer_sub_keys[l].astype(BF16), tl=tm)
        w, ii, jj = _peer_act(u2, peer_u[l].astype(BF16), ii_t, jj_t, gg_t, tb=tp, groups=32)
        xf = _peer_out(w, ii, jj, peer_v[l].astype(BF16), x1, g2, ln2_g[l], ln2_b[l],
                       seq=seq, alpha=alpha, tb=tm, groups=16)
    return xf.reshape(bsz, seq, d)
```

```python
import functools
import math

import jax
import jax.numpy as jnp
import numpy as np
from jax import lax
from jax.experimental import pallas as pl
from jax.experimental.pallas import tpu as pltpu

F32 = jnp.float32
BF16 = jnp.bfloat16
I32 = jnp.int32

LN_EPS = 1e-5
PEER_TOPK = 16
LANES = 128
VMEM_LIMIT_BYTES = 56 * 1024 * 1024
NEG_INF = float("-inf")
LOG2_E = math.log2(math.e)


def _params(*semantics):
    return pltpu.CompilerParams(dimension_semantics=semantics, vmem_limit_bytes=VMEM_LIMIT_BYTES)


def _normalize(x):
    mu = jnp.mean(x, axis=-1, keepdims=True)
    xc = x - mu
    var = jnp.mean(xc * xc, axis=-1, keepdims=True)
    return xc * lax.rsqrt(var + LN_EPS)


def _resident(shape):
    nd = len(shape)
    return pl.BlockSpec(shape, lambda *_: (0,) * nd)


def _mod_kernel(c_ref, w_ref, b_ref, o_ref):
    c = c_ref[...]
    cs = c * jax.nn.sigmoid(c)
    o_ref[...] = jnp.dot(cs, w_ref[...], preferred_element_type=F32,
                         precision=lax.Precision.HIGHEST) + b_ref[...]


def _mod(c, w, b):
    bsz, d = c.shape
    n = w.shape[1]
    rows = -(-bsz // 8) * 8
    tn = math.gcd(n, 1536)
    cp = jnp.pad(c, ((0, rows - bsz), (0, 0)))
    out = pl.pallas_call(
        _mod_kernel,
        grid=(n // tn,),
        in_specs=[pl.BlockSpec((rows, d), lambda j: (0, 0)),
                  pl.BlockSpec((d, tn), lambda j: (0, j)),
                  pl.BlockSpec((1, tn), lambda j: (0, j))],
        out_specs=pl.BlockSpec((rows, tn), lambda j: (0, j)),
        out_shape=jax.ShapeDtypeStruct((rows, n), F32),
        compiler_params=_params("arbitrary"),
        name="mod",
    )(cp, w, b.reshape(1, n))
    return out[:bsz]


def _inproj_kernel(x_ref, sh_ref, sc_ref, wqkv_ref, wf_ref, wc_ref, wg_ref,
                   qkv_ref, f_ref, h_ref, g_ref, *, att_w, conv_c, qscale):
    y = _normalize(x_ref[...])
    u = (y * (1.0 + sc_ref[...]) + sh_ref[...]).astype(BF16)
    qkv = jnp.dot(u, wqkv_ref[...], preferred_element_type=F32)
    qkv_ref[:, :att_w] = (qkv[:, :att_w] * qscale).astype(BF16)
    qkv_ref[:, att_w:] = qkv[:, att_w:].astype(BF16)
    f_ref[...] = jnp.dot(u, wf_ref[...], preferred_element_type=F32)
    cab = jnp.dot(u, wc_ref[...], preferred_element_type=F32)
    h_ref[...] = cab[:, :conv_c] * jax.nn.sigmoid(cab[:, conv_c:])
    g_ref[...] = jax.nn.sigmoid(jnp.dot(u, wg_ref[...], preferred_element_type=F32))


def _inproj(x2, sh, sc, wqkv, wf, wc, wg, *, seq, att_w, conv_c, head_dim, tm):
    t, d = x2.shape
    tpb = seq // tm
    row = lambda i: (i, 0)
    per_batch = pl.BlockSpec((None, 1, d), lambda i: (i // tpb, 0, 0))
    kern = functools.partial(_inproj_kernel, att_w=att_w, conv_c=conv_c,
                             qscale=LOG2_E / math.sqrt(head_dim))
    return pl.pallas_call(
        kern,
        grid=(t // tm,),
        in_specs=[pl.BlockSpec((tm, d), row), per_batch, per_batch,
                  _resident(wqkv.shape), _resident(wf.shape), _resident(wc.shape),
                  _resident(wg.shape)],
        out_specs=[pl.BlockSpec((tm, 3 * att_w), row), pl.BlockSpec((tm, LANES), row),
                   pl.BlockSpec((tm, conv_c), row), pl.BlockSpec((tm, 2 * d), row)],
        out_shape=[jax.ShapeDtypeStruct((t, 3 * att_w), BF16),
                   jax.ShapeDtypeStruct((t, LANES), F32),
                   jax.ShapeDtypeStruct((t, conv_c), F32),
                   jax.ShapeDtypeStruct((t, 2 * d), F32)],
        compiler_params=_params("arbitrary"),
        name="inproj",
    )(x2, sh, sc, wqkv, wf, wc, wg)


BIAS_TERMS = 3


def _fcum_kernel(f_ref, bf_ref, qb_ref, kb_ref, carry_ref, *, tc, n_heads):
    @pl.when(pl.program_id(1) == 0)
    def _():
        carry_ref[...] = jnp.zeros(carry_ref.shape, F32)

    z = f_ref[...] + bf_ref[...]
    logf = jnp.minimum(z, 0.0) - jnp.log1p(jnp.exp(-jnp.abs(z)))
    r = lax.broadcasted_iota(I32, (tc, tc), 0)
    c = lax.broadcasted_iota(I32, (tc, tc), 1)
    tri = jnp.where(c <= r, 1.0, 0.0).astype(F32)
    cs = jnp.dot(tri, logf, preferred_element_type=F32,
                 precision=lax.Precision.HIGHEST) + carry_ref[...]
    carry_ref[...] = cs[tc - 1:tc, :]
    rem = cs * LOG2_E
    terms = []
    for _ in range(BIAS_TERMS):
        term = rem.astype(BF16)
        terms.append(term.astype(F32))
        rem = rem - terms[-1]
    lane = lax.broadcasted_iota(I32, (tc, LANES), 1)
    for hh in range(n_heads):
        qb = jnp.where((lane >= BIAS_TERMS) & (lane < 2 * BIAS_TERMS), -1.0, 0.0)
        kb = jnp.where(lane < BIAS_TERMS, 1.0, 0.0)
        for r, term in enumerate(terms):
            col = jnp.broadcast_to(term[:, hh:hh + 1], (tc, LANES))
            qb = jnp.where(lane == r, col, qb)
            kb = jnp.where(lane == BIAS_TERMS + r, col, kb)
        qb_ref[:, hh * LANES:(hh + 1) * LANES] = qb.astype(BF16)
        kb_ref[:, hh * LANES:(hh + 1) * LANES] = kb.astype(BF16)


def _fcum(f3, bf_row, *, n_heads, tc):
    bsz, seq, w = f3.shape
    slab = pl.BlockSpec((None, tc, n_heads * LANES), lambda b, j: (b, j, 0))
    out = jax.ShapeDtypeStruct((bsz, seq, n_heads * LANES), BF16)
    return pl.pallas_call(
        functools.partial(_fcum_kernel, tc=tc, n_heads=n_heads),
        grid=(bsz, seq // tc),
        in_specs=[pl.BlockSpec((None, tc, w), lambda b, j: (b, j, 0)),
                  pl.BlockSpec((1, w), lambda b, j: (0, 0))],
        out_specs=[slab, slab],
        out_shape=[out, out],
        scratch_shapes=[pltpu.VMEM((1, w), F32)],
        compiler_params=_params("arbitrary", "arbitrary"),
        name="fcum",
    )(f3, bf_row)


ATTN_MASKED = 1
ATTN_LAST = 2


def _attn_tables(nq, tq, tk):
    qi_l, ki_l, fl_l = [], [], []
    for qi in range(nq):
        last = (qi * tq + tq - 1) // tk
        for ki in range(last + 1):
            masked = ki * tk + tk - 1 > qi * tq
            qi_l.append(qi)
            ki_l.append(ki)
            fl_l.append((ATTN_MASKED if masked else 0) | (ATTN_LAST if ki == last else 0))
    as_i32 = lambda v: jnp.asarray(np.asarray(v, np.int32))
    return as_i32(qi_l), as_i32(ki_l), as_i32(fl_l)


def _attn_kernel(qi_tab, ki_tab, fl_tab, q_ref, qb_ref, k_ref, kb_ref, v_ref, o_ref,
                 qop_sc, m_sc, l_sc, acc_sc, *, tq, tk, rb, dh, heads):
    p = pl.program_id(2)
    qi = qi_tab[p]
    ki = ki_tab[p]
    fl = fl_tab[p]
    head_of_lane = lax.broadcasted_iota(I32, (tq, LANES), 1) // dh

    @pl.when(ki == 0)
    def _():
        m_sc[...] = jnp.full(m_sc.shape, NEG_INF, F32)
        l_sc[...] = jnp.zeros(l_sc.shape, F32)
        acc_sc[...] = jnp.zeros(acc_sc.shape, F32)
        q = q_ref[...]
        for hh in range(heads):
            qop_sc[hh, :, :LANES] = jnp.where(head_of_lane == hh, q, jnp.zeros_like(q))
            qop_sc[hh, :, LANES:] = qb_ref[:, hh * LANES:(hh + 1) * LANES]

    def per_head(vals):
        out = vals[0]
        lane_head = lax.broadcasted_iota(I32, out.shape, 1) // dh
        for hh in range(1, heads):
            out = jnp.where(lane_head == hh, vals[hh], out)
        return out

    def step(diagonal):
        for r0 in range(0, tq, rb):
            rows = slice(r0, r0 + rb)
            nk = min(tk, r0 + rb) if diagonal else tk
            if diagonal:
                keep = (lax.broadcasted_iota(I32, (rb, nk), 1)
                        <= r0 + lax.broadcasted_iota(I32, (rb, nk), 0))
            alphas, pvs = [], []
            for hh in range(heads):
                kop = jnp.concatenate([k_ref[:nk, :], kb_ref[:nk, hh * LANES:(hh + 1) * LANES]],
                                      axis=1)
                s = lax.dot_general(qop_sc[hh, rows, :], kop, (((1,), (1,)), ((), ())),
                                    preferred_element_type=F32)
                if diagonal:
                    s = jnp.where(keep, s, NEG_INF)
                m_prev = m_sc[hh, rows, :]
                m_new = jnp.maximum(m_prev, jnp.max(s, axis=-1, keepdims=True))
                alpha = jnp.exp2(m_prev - m_new)
                pr = jnp.exp2(s - jnp.concatenate([m_new] * (nk // LANES), axis=1))
                l_sc[hh, rows, :] = alpha * l_sc[hh, rows, :] + jnp.sum(pr, axis=-1, keepdims=True)
                m_sc[hh, rows, :] = m_new
                alphas.append(alpha)
                pvs.append(jnp.dot(pr.astype(BF16), v_ref[:nk, :], preferred_element_type=F32))
            acc_sc[rows, :] = per_head(alphas) * acc_sc[rows, :] + per_head(pvs)

    pl.when((fl & ATTN_MASKED) != 0)(lambda: step(True))
    pl.when((fl & ATTN_MASKED) == 0)(lambda: step(False))

    @pl.when((fl & ATTN_LAST) != 0)
    def _():
        o_ref[...] = (acc_sc[...] / per_head([l_sc[hh] for hh in range(heads)])).astype(o_ref.dtype)


def _attention(qkv, qb, kb, *, bsz, seq, att_w, head_dim, tq, tk, rb):
    heads = LANES // head_dim
    ncol = att_w // LANES
    nq, nk = seq // tq, seq // tk
    assert tq == tk and tq % rb == 0 and rb % LANES == 0
    qi_tab, ki_tab, fl_tab = _attn_tables(nq, tq, tk)
    kern = functools.partial(_attn_kernel, tq=tq, tk=tk, rb=rb, dh=head_dim, heads=heads)
    qrow = lambda b, c, p, qt, kt, ft: b * nq + qt[p]
    krow = lambda b, c, p, qt, kt, ft: b * nk + kt[p]
    grid_spec = pltpu.PrefetchScalarGridSpec(
        num_scalar_prefetch=3,
        grid=(bsz, ncol, int(qi_tab.shape[0])),
        in_specs=[
            pl.BlockSpec((tq, LANES), lambda b, c, *a: (qrow(b, c, *a), c)),
            pl.BlockSpec((tq, heads * LANES), lambda b, c, *a: (qrow(b, c, *a), c)),
            pl.BlockSpec((tk, LANES), lambda b, c, *a: (krow(b, c, *a), ncol + c)),
            pl.BlockSpec((tk, heads * LANES), lambda b, c, *a: (krow(b, c, *a), c)),
            pl.BlockSpec((tk, LANES), lambda b, c, *a: (krow(b, c, *a), 2 * ncol + c)),
        ],
        out_specs=pl.BlockSpec((tq, LANES), lambda b, c, *a: (qrow(b, c, *a), c)),
        scratch_shapes=[pltpu.VMEM((heads, tq, 2 * LANES), BF16),
                        pltpu.VMEM((heads, tq, LANES), F32), pltpu.VMEM((heads, tq, LANES), F32),
                        pltpu.VMEM((tq, LANES), F32)],
    )
    return pl.pallas_call(
        kern,
        grid_spec=grid_spec,
        out_shape=jax.ShapeDtypeStruct((bsz * seq, att_w), BF16),
        compiler_params=_params("arbitrary", "arbitrary", "arbitrary"),
        name="attn",
    )(qi_tab, ki_tab, fl_tab, qkv, qb, qkv, kb, qkv)


CONV_HALO = 32
CONV_CHUNK = 64


SUBLANES = 8


def _conv_kernel(h_ref, halo_ref, w_ref, b_ref, g_ref, beta_ref, o_ref, sh_sc, *, ts, taps, tpb):
    first = pl.program_id(0) % tpb == 0
    rows = ts + CONV_HALO - SUBLANES
    sh_sc[0, 0:CONV_HALO, :] = jnp.where(first, 0.0, halo_ref[...])
    sh_sc[0, CONV_HALO:, :] = h_ref[...]
    for r in range(1, SUBLANES):
        sh_sc[r, 0:rows, :] = sh_sc[0, r:r + rows, :]
    lead = CONV_HALO - (taps - 1)
    for c0 in range(0, ts, CONV_CHUNK):
        acc = jnp.broadcast_to(b_ref[...], (CONV_CHUNK, b_ref.shape[1]))
        for k in range(taps):
            r = (lead + k) % SUBLANES
            start = c0 + lead + k - r
            acc = acc + w_ref[k:k + 1, :] * sh_sc[r, start:start + CONV_CHUNK, :]
        y = _normalize(acc) * g_ref[...] + beta_ref[...]
        o_ref[c0:c0 + CONV_CHUNK, :] = (y * jax.nn.sigmoid(y)).astype(o_ref.dtype)


def _conv(h, w, b, g, beta, *, seq, ts):
    t, c = h.shape
    taps = w.shape[0]
    assert taps - 1 <= CONV_HALO and ts % CONV_CHUNK == 0 and ts % CONV_HALO == 0
    tpb = seq // ts
    hb = ts // CONV_HALO
    vec = lambda v: v.reshape(1, c)
    return pl.pallas_call(
        functools.partial(_conv_kernel, ts=ts, taps=taps, tpb=tpb),
        grid=(t // ts,),
        in_specs=[pl.BlockSpec((ts, c), lambda i: (i, 0)),
                  pl.BlockSpec((CONV_HALO, c), lambda i: (jnp.maximum(i * hb - 1, 0), 0)),
                  _resident((taps, c)), _resident((1, c)), _resident((1, c)), _resident((1, c))],
        out_specs=pl.BlockSpec((ts, c), lambda i: (i, 0)),
        out_shape=jax.ShapeDtypeStruct((t, c), BF16),
        scratch_shapes=[pltpu.VMEM((SUBLANES, ts + CONV_HALO, c), F32)],
        compiler_params=_params("arbitrary"),
        name="conv",
    )(h, h, w, vec(b), vec(g), vec(beta))


def _merge_kernel(attn_ref, hc_ref, g_ref, x_ref, g1_ref, sh2_ref, sc2_ref, wa_ref, wc_ref,
                  wo_ref, lg_ref, lb_ref, wq_ref, x1_ref, u2_ref, qp_ref, *, alpha):
    d = x_ref.shape[1]
    ya = jnp.dot(attn_ref[...], wa_ref[...], preferred_element_type=F32)
    yb = jnp.dot(hc_ref[...], wc_ref[...], preferred_element_type=F32)
    merged = (g_ref[:, :d] * ya + g_ref[:, d:] * yb).astype(BF16)
    out = jnp.dot(merged, wo_ref[...], preferred_element_type=F32)
    x1 = _normalize(alpha * x_ref[...] + (1.0 + g1_ref[...]) * out) * lg_ref[...] + lb_ref[...]
    x1_ref[...] = x1
    u2 = (_normalize(x1) * (1.0 + sc2_ref[...]) + sh2_ref[...]).astype(BF16)
    u2_ref[...] = u2
    qp_ref[...] = jnp.dot(u2, wq_ref[...], preferred_element_type=F32).astype(qp_ref.dtype)


def _merge(attn, hc, gates, x2, g1, sh2, sc2, wa, wc, wo, lg, lb, wq, *, seq, alpha, tm):
    t, d = x2.shape
    tpb = seq // tm
    row = lambda i: (i, 0)
    per_batch = pl.BlockSpec((None, 1, d), lambda i: (i // tpb, 0, 0))
    nq = wq.shape[1]
    return pl.pallas_call(
        functools.partial(_merge_kernel, alpha=alpha),
        grid=(t // tm,),
        in_specs=[pl.BlockSpec((tm, attn.shape[1]), row), pl.BlockSpec((tm, hc.shape[1]), row),
                  pl.BlockSpec((tm, 2 * d), row), pl.BlockSpec((tm, d), row),
                  per_batch, per_batch, per_batch,
                  _resident(wa.shape), _resident(wc.shape), _resident(wo.shape),
                  _resident((1, d)), _resident((1, d)), _resident(wq.shape)],
        out_specs=[pl.BlockSpec((tm, d), row), pl.BlockSpec((tm, d), row),
                   pl.BlockSpec((tm, nq), row)],
        out_shape=[jax.ShapeDtypeStruct((t, d), F32), jax.ShapeDtypeStruct((t, d), BF16),
                   jax.ShapeDtypeStruct((t, nq), BF16)],
        compiler_params=_params("arbitrary"),
        name="merge",
    )(attn, hc, gates, x2, g1, sh2, sc2, wa, wc, wo, lg.reshape(1, d), lb.reshape(1, d), wq)


def _staircase(k):
    return [(a, b) for a in range(k) for b in range(k) if (a + 1) * (b + 1) <= k]


ID_PAD = 1e9


def _topk_columns(s, k, ids=None):
    n, tl = s.shape
    ng = n // SUBLANES
    vals = [s[SUBLANES * g:SUBLANES * (g + 1), :] for g in range(ng)]
    if ids is None:
        sub = lax.broadcasted_iota(I32, (SUBLANES, tl), 0).astype(F32)
        ids = [sub + float(SUBLANES * g) for g in range(ng)]
    else:
        ids = [ids[SUBLANES * g:SUBLANES * (g + 1), :] for g in range(ng)]
    for phase in range(ng):
        for a in range(phase % 2, ng - 1, 2):
            up = vals[a + 1] > vals[a]
            vals[a], vals[a + 1] = (jnp.maximum(vals[a], vals[a + 1]),
                                    jnp.minimum(vals[a], vals[a + 1]))
            ids[a], ids[a + 1] = (jnp.where(up, ids[a + 1], ids[a]),
                                  jnp.where(up, ids[a], ids[a + 1]))
    out_v, out_i = [], []
    for r in range(k):
        m = jnp.max(vals[0], axis=0, keepdims=True)
        i = jnp.min(jnp.where(vals[0] == m, ids[0], ID_PAD), axis=0, keepdims=True)
        out_v.append(m)
        out_i.append(i)
        win = ids[0] == i
        live = min(ng, k - r)
        for g in range(live - 1):
            vals[g] = jnp.where(win, vals[g + 1], vals[g])
            ids[g] = jnp.where(win, ids[g + 1], ids[g])
        if live == ng:
            vals[ng - 1] = jnp.where(win, NEG_INF, vals[ng - 1])
    return jnp.concatenate(out_v, axis=0), jnp.concatenate(out_i, axis=0)


RETRIEVE_HEADS = 2


def _retrieve_kernel(qp_ref, keys_ref, flat_ref, ii_ref, jj_ref, gg_ref, *, dhalf, nkeys, topk):
    tl = qp_ref.shape[0]
    qb = qp_ref[...]
    pairs = _staircase(topk)
    npad = flat_ref.shape[0]
    for hh in range(RETRIEVE_HEADS):
        sv, si = [], []
        for half in range(2):
            col = (2 * hh + half) * dhalf
            s = lax.dot_general(keys_ref[half], qb[:, col:col + dhalf],
                                (((1,), (1,)), ((), ())), preferred_element_type=F32)
            v, i = _topk_columns(s, topk)
            sv.append(v)
            si.append(i)

        cand = [sv[0][a:a + 1, :] + sv[1][b:b + 1, :] for a, b in pairs]
        cand += [jnp.full((1, tl), NEG_INF, F32)] * (npad - len(pairs))
        cand = jnp.concatenate(cand, axis=0)
        cv, ci = _topk_columns(cand, topk, flat_ref[...])

        ci = ci.astype(I32)
        ia = lax.shift_right_logical(ci, int(math.log2(topk)))
        ib = ci & (topk - 1)
        ii = jnp.zeros((topk, tl), F32)
        jj = jnp.zeros((topk, tl), F32)
        for a in range(topk):
            ii = jnp.where(ia == a, si[0][a:a + 1, :], ii)
            jj = jnp.where(ib == a, si[1][a:a + 1, :], jj)
        rows = slice(hh * topk, (hh + 1) * topk)
        ii_ref[rows, :] = ii.astype(I32)
        jj_ref[rows, :] = jj.astype(I32)
        e = jnp.exp(cv - cv[0:1, :])
        gg_ref[rows, :] = e / jnp.sum(e, axis=0, keepdims=True)


def _retrieve(qp, keys, *, tl):
    t, nq = qp.shape
    _, nkeys, dhalf = keys.shape
    heads = nq // (2 * dhalf)
    assert 2 * dhalf == LANES and PEER_TOPK & (PEER_TOPK - 1) == 0 and heads % RETRIEVE_HEADS == 0
    pairs = _staircase(PEER_TOPK)
    npad = -(-len(pairs) // 8) * 8
    flat = np.full((npad, tl), ID_PAD / 2, np.float32)
    flat[:len(pairs), :] = np.asarray([a * PEER_TOPK + b for a, b in pairs], np.float32)[:, None]
    kern = functools.partial(_retrieve_kernel, dhalf=dhalf, nkeys=nkeys, topk=PEER_TOPK)
    out = jax.ShapeDtypeStruct((heads * PEER_TOPK, t), I32)
    blk = pl.BlockSpec((RETRIEVE_HEADS * PEER_TOPK, tl), lambda i, h: (h, i))
    return pl.pallas_call(
        kern,
        grid=(t // tl, heads // RETRIEVE_HEADS),
        in_specs=[pl.BlockSpec((tl, RETRIEVE_HEADS * LANES), lambda i, h: (i, h)),
                  _resident(keys.shape),
                  _resident(flat.shape)],
        out_specs=[blk, blk, blk],
        out_shape=[out, out, jax.ShapeDtypeStruct((heads * PEER_TOPK, t), F32)],
        compiler_params=_params("arbitrary", "arbitrary"),
        name="retrieve",
    )(qp, keys, jnp.asarray(flat))


def _gelu_tanh(a):
    return 0.5 * a * (1.0 + jnp.tanh(math.sqrt(2.0 / math.pi) * (a + 0.044715 * a * a * a)))


def _peer_act_kernel(x_ref, u_ref, iit_ref, jjt_ref, ggt_ref, w_ref, ii_ref, jj_ref,
                     *, groups, nchunks):
    c = pl.program_id(1)

    @pl.when(c == 0)
    def _():
        ii_ref[...] = iit_ref[...].T
        jj_ref[...] = jjt_ref[...].T
        w_ref[...] = jnp.zeros(w_ref.shape, F32)

    a_all = lax.dot_general(x_ref[...], u_ref[...], (((1,), (1,)), ((), ())),
                            preferred_element_type=F32)
    ii = ii_ref[...]
    jj = jj_ref[...]
    acc = w_ref[...]
    for g in range(groups):
        picked = jnp.take_along_axis(a_all[:, g * LANES:(g + 1) * LANES], jj, axis=1)
        acc = jnp.where(ii == c * groups + g, picked, acc)
    w_ref[...] = acc

    @pl.when(c == nchunks - 1)
    def _():
        w_ref[...] = _gelu_tanh(w_ref[...]) * ggt_ref[...].T


def _peer_act(u2, utab, ii_t, jj_t, gg_t, *, tb, groups):
    t, d = u2.shape
    ne = utab.shape[0]
    hk = ii_t.shape[0]
    ce = groups * LANES
    nchunks = ne // ce
    tok = lambda i, c: (i, 0)
    hk_major = pl.BlockSpec((hk, tb), lambda i, c: (0, i))
    return pl.pallas_call(
        functools.partial(_peer_act_kernel, groups=groups, nchunks=nchunks),
        grid=(t // tb, nchunks),
        in_specs=[pl.BlockSpec((tb, d), tok), pl.BlockSpec((ce, d), lambda i, c: (c, 0)),
                  hk_major, hk_major, hk_major],
        out_specs=[pl.BlockSpec((tb, hk), tok)] * 3,
        out_shape=[jax.ShapeDtypeStruct((t, hk), F32), jax.ShapeDtypeStruct((t, hk), I32),
                   jax.ShapeDtypeStruct((t, hk), I32)],
        compiler_params=_params("arbitrary", "arbitrary"),
        name="peer_act",
    )(u2, utab, ii_t, jj_t, gg_t)


SCATTER_UNROLL = 32


def _w_pitch(tb):
    p = -(-tb // 8)
    return 8 * (p if p % 2 else p + 1)


def _peer_out_kernel(w_ref, ii_ref, jj_ref, v_ref, x1_ref, g2_ref, lg_ref, lb_ref,
                     o_ref, w_sc, acc_sc, *, tb, groups, nchunks, alpha):
    c = pl.program_id(1)
    pitch = _w_pitch(tb)

    @pl.when(c == 0)
    def _():
        acc_sc[...] = jnp.zeros(acc_sc.shape, F32)
        sub = lax.broadcasted_iota(I32, (LANES, LANES), 0)

        def scatter(tok, carry):
            wrow = w_ref[pl.ds(tok, 1), :]
            lhs = jnp.where(sub == ii_ref[pl.ds(tok, 1), :], wrow, 0.0).astype(BF16)
            rhs = jnp.where(sub == jj_ref[pl.ds(tok, 1), :], 1.0, 0.0).astype(BF16)
            w_sc[pl.ds(tok, LANES, stride=pitch), :] = lax.dot_general(
                lhs, rhs, (((1,), (1,)), ((), ())), preferred_element_type=F32)
            return carry

        lax.fori_loop(0, tb, scatter, 0, unroll=SCATTER_UNROLL)

    parts = [w_sc[pl.ds(pl.multiple_of((c * groups + g) * pitch, 8), tb), :].astype(BF16)
             for g in range(groups)]
    acc_sc[...] += jnp.dot(jnp.concatenate(parts, axis=1), v_ref[...],
                           preferred_element_type=F32)

    @pl.when(c == nchunks - 1)
    def _():
        r = alpha * x1_ref[...] + (1.0 + g2_ref[...]) * acc_sc[...]
        o_ref[...] = _normalize(r) * lg_ref[...] + lb_ref[...]


def _peer_out(w, ii, jj, vtab, x1, g2, lg, lb, *, seq, alpha, tb, groups):
    t, d = x1.shape
    ne = vtab.shape[0]
    assert ne == LANES * LANES
    ce = groups * LANES
    nchunks = ne // ce
    tpb = seq // tb
    tok = lambda i, c: (i, 0)
    kern = functools.partial(_peer_out_kernel, tb=tb, groups=groups, nchunks=nchunks, alpha=alpha)
    return pl.pallas_call(
        kern,
        grid=(t // tb, nchunks),
        in_specs=[pl.BlockSpec((tb, LANES), tok), pl.BlockSpec((tb, LANES), tok),
                  pl.BlockSpec((tb, LANES), tok),
                  pl.BlockSpec((ce, d), lambda i, c: (c, 0)),
                  pl.BlockSpec((tb, d), tok, pipeline_mode=pl.Buffered(1)),
                  pl.BlockSpec((None, 1, d), lambda i, c: (i // tpb, 0, 0)),
                  _resident((1, d)), _resident((1, d))],
        out_specs=pl.BlockSpec((tb, d), tok),
        out_shape=jax.ShapeDtypeStruct((t, d), F32),
        scratch_shapes=[pltpu.VMEM((LANES * _w_pitch(tb), LANES), F32), pltpu.VMEM((tb, d), F32)],
        compiler_params=_params("arbitrary", "arbitrary"),
        name="peer_out",
    )(w, ii, jj, vtab, x1, g2, lg.reshape(1, d), lb.reshape(1, d))


def _tile(n, want):
    t = min(n, want)
    while n % t:
        t -= 1
    return t


def kernel(x, c, w_mod, b_mod, w_in, b_f, conv_w, conv_b, conv_ln_g, conv_ln_b,
           w_attn_out, w_conv_out, w_out, ln1_g, ln1_b, peer_wq, peer_sub_keys,
           peer_u, peer_v, ln2_g, ln2_b):
    bsz, seq, d = x.shape
    depth = w_mod.shape[0]
    t = bsz * seq
    att_w = w_attn_out.shape[1]
    n_heads = b_f.shape[1]
    head_dim = att_w // n_heads
    conv_c = conv_w.shape[2]
    alpha = (2.0 * depth) ** 0.25
    assert n_heads <= LANES and LANES % head_dim == 0 and att_w % LANES == 0

    tm = _tile(seq, 512)
    ta = _tile(seq, 2048)
    tp = _tile(seq, 1024)
    xf = x.reshape(t, d)
    for l in range(depth):
        mod = _mod(c, w_mod[l], b_mod[l])
        sh1, sc1, g1, sh2, sc2, g2 = [m.reshape(bsz, 1, d) for m in jnp.split(mod, 6, axis=-1)]

        o_f = 3 * att_w
        o_c = o_f + n_heads
        o_g = o_c + 2 * conv_c
        wl = w_in[l]
        wqkv = wl[:, :o_f].astype(BF16)
        wf = jnp.pad(wl[:, o_f:o_c], ((0, 0), (0, LANES - n_heads))).astype(BF16)
        wc = wl[:, o_c:o_g].astype(BF16)
        wg = wl[:, o_g:].astype(BF16)
        qkv, f, h, gates = _inproj(xf, sh1, sc1, wqkv, wf, wc, wg, seq=seq, att_w=att_w,
                                   conv_c=conv_c, head_dim=head_dim, tm=tm)

        bf_row = jnp.pad(b_f[l], (0, LANES - n_heads)).reshape(1, LANES)
        qb, kb = _fcum(f.reshape(bsz, seq, LANES), bf_row, n_heads=n_heads, tc=_tile(seq, 256))
        attn = _attention(qkv, qb.reshape(t, -1), kb.reshape(t, -1), bsz=bsz, seq=seq,
                          att_w=att_w, head_dim=head_dim, tq=ta, tk=ta, rb=_tile(ta, 512))

        hc = _conv(h, conv_w[l], conv_b[l], conv_ln_g[l], conv_ln_b[l], seq=seq, ts=tm)

        x1, u2, qp = _merge(attn, hc, gates, xf, g1, sh2, sc2,
                            w_attn_out[l].astype(BF16), w_conv_out[l].astype(BF16),
                            w_out[l].astype(BF16), ln1_g[l], ln1_b[l],
                            peer_wq[l].astype(BF16), seq=seq, alpha=alpha, tm=tm)

        ii_t, jj_t, gg_t = _retrieve(qp, peer_sub_keys[l].astype(BF16), tl=tm)
        w, ii, jj = _peer_act(u2, peer_u[l].astype(BF16), ii_t, jj_t, gg_t, tb=tp, groups=32)
        xf = _peer_out(w, ii, jj, peer_v[l].astype(BF16), x1, g2, ln2_g[l], ln2_b[l],
                       seq=seq, alpha=alpha, tb=tm, groups=16)
    return xf.reshape(bsz, seq, d)
```

```python
import functools
import math

import jax
import jax.numpy as jnp
import numpy as np
from jax import lax
from jax.experimental import pallas as pl
from jax.experimental.pallas import tpu as pltpu

F32 = jnp.float32
BF16 = jnp.bfloat16
I32 = jnp.int32

LN_EPS = 1e-5
PEER_TOPK = 16
LANES = 128
VMEM_LIMIT_BYTES = 56 * 1024 * 1024
NEG_INF = float("-inf")
LOG2_E = math.log2(math.e)


def _params(*semantics):
    return pltpu.CompilerParams(dimension_semantics=semantics, vmem_limit_bytes=VMEM_LIMIT_BYTES)


def _normalize(x):
    mu = jnp.mean(x, axis=-1, keepdims=True)
    xc = x - mu
    var = jnp.mean(xc * xc, axis=-1, keepdims=True)
    return xc * lax.rsqrt(var + LN_EPS)


def _resident(shape):
    nd = len(shape)
    return pl.BlockSpec(shape, lambda *_: (0,) * nd)


def _mod_kernel(c_ref, w_ref, b_ref, o_ref):
    c = c_ref[...]
    cs = c * jax.nn.sigmoid(c)
    o_ref[...] = jnp.dot(cs, w_ref[...], preferred_element_type=F32,
                         precision=lax.Precision.HIGHEST) + b_ref[...]


def _mod(c, w, b):
    bsz, d = c.shape
    n = w.shape[1]
    rows = -(-bsz // 8) * 8
    tn = math.gcd(n, 1536)
    cp = jnp.pad(c, ((0, rows - bsz), (0, 0)))
    out = pl.pallas_call(
        _mod_kernel,
        grid=(n // tn,),
        in_specs=[pl.BlockSpec((rows, d), lambda j: (0, 0)),
                  pl.BlockSpec((d, tn), lambda j: (0, j)),
                  pl.BlockSpec((1, tn), lambda j: (0, j))],
        out_specs=pl.BlockSpec((rows, tn), lambda j: (0, j)),
        out_shape=jax.ShapeDtypeStruct((rows, n), F32),
        compiler_params=_params("arbitrary"),
        name="mod",
    )(cp, w, b.reshape(1, n))
    return out[:bsz]


def _inproj_kernel(x_ref, sh_ref, sc_ref, wqkv_ref, wf_ref, wc_ref, wg_ref,
                   qkv_ref, f_ref, h_ref, g_ref, *, att_w, conv_c, qscale):
    y = _normalize(x_ref[...])
    u = (y * (1.0 + sc_ref[...]) + sh_ref[...]).astype(BF16)
    qkv = jnp.dot(u, wqkv_ref[...], preferred_element_type=F32)
    qkv_ref[:, :att_w] = (qkv[:, :att_w] * qscale).astype(BF16)
    qkv_ref[:, att_w:] = qkv[:, att_w:].astype(BF16)
    f_ref[...] = jnp.dot(u, wf_ref[...], preferred_element_type=F32)
    cab = jnp.dot(u, wc_ref[...], preferred_element_type=F32)
    h_ref[...] = cab[:, :conv_c] * jax.nn.sigmoid(cab[:, conv_c:])
    g_ref[...] = jax.nn.sigmoid(jnp.dot(u, wg_ref[...], preferred_element_type=F32))


def _inproj(x2, sh, sc, wqkv, wf, wc, wg, *, seq, att_w, conv_c, head_dim, tm):
    t, d = x2.shape
    tpb = seq // tm
    row = lambda i: (i, 0)
    per_batch = pl.BlockSpec((None, 1, d), lambda i: (i // tpb, 0, 0))
    kern = functools.partial(_inproj_kernel, att_w=att_w, conv_c=conv_c,
                             qscale=LOG2_E / math.sqrt(head_dim))
    return pl.pallas_call(
        kern,
        grid=(t // tm,),
        in_specs=[pl.BlockSpec((tm, d), row), per_batch, per_batch,
                  _resident(wqkv.shape), _resident(wf.shape), _resident(wc.shape),
                  _resident(wg.shape)],
        out_specs=[pl.BlockSpec((tm, 3 * att_w), row), pl.BlockSpec((tm, LANES), row),
                   pl.BlockSpec((tm, conv_c), row), pl.BlockSpec((tm, 2 * d), row)],
        out_shape=[jax.ShapeDtypeStruct((t, 3 * att_w), BF16),
                   jax.ShapeDtypeStruct((t, LANES), F32),
                   jax.ShapeDtypeStruct((t, conv_c), F32),
                   jax.ShapeDtypeStruct((t, 2 * d), F32)],
        compiler_params=_params("arbitrary"),
        name="inproj",
    )(x2, sh, sc, wqkv, wf, wc, wg)


BIAS_TERMS = 3


def _fcum_kernel(f_ref, bf_ref, qb_ref, kb_ref, carry_ref, *, tc, n_heads):
    @pl.when(pl.program_id(1) == 0)
    def _():
        carry_ref[...] = jnp.zeros(carry_ref.shape, F32)

    z = f_ref[...] + bf_ref[...]
    logf = jnp.minimum(z, 0.0) - jnp.log1p(jnp.exp(-jnp.abs(z)))
    r = lax.broadcasted_iota(I32, (tc, tc), 0)
    c = lax.broadcasted_iota(I32, (tc, tc), 1)
    tri = jnp.where(c <= r, 1.0, 0.0).astype(F32)
    cs = jnp.dot(tri, logf, preferred_element_type=F32,
                 precision=lax.Precision.HIGHEST) + carry_ref[...]
    carry_ref[...] = cs[tc - 1:tc, :]
    rem = cs * LOG2_E
    terms = []
    for _ in range(BIAS_TERMS):
        term = rem.astype(BF16)
        terms.append(term.astype(F32))
        rem = rem - terms[-1]
    lane = lax.broadcasted_iota(I32, (tc, LANES), 1)
    for hh in range(n_heads):
        qb = jnp.where((lane >= BIAS_TERMS) & (lane < 2 * BIAS_TERMS), -1.0, 0.0)
        kb = jnp.where(lane < BIAS_TERMS, 1.0, 0.0)
        for r, term in enumerate(terms):
            col = jnp.broadcast_to(term[:, hh:hh + 1], (tc, LANES))
            qb = jnp.where(lane == r, col, qb)
            kb = jnp.where(lane == BIAS_TERMS + r, col, kb)
        qb_ref[:, hh * LANES:(hh + 1) * LANES] = qb.astype(BF16)
        kb_ref[:, hh * LANES:(hh + 1) * LANES] = kb.astype(BF16)


def _fcum(f3, bf_row, *, n_heads, tc):
    bsz, seq, w = f3.shape
    slab = pl.BlockSpec((None, tc, n_heads * LANES), lambda b, j: (b, j, 0))
    out = jax.ShapeDtypeStruct((bsz, seq, n_heads * LANES), BF16)
    return pl.pallas_call(
        functools.partial(_fcum_kernel, tc=tc, n_heads=n_heads),
        grid=(bsz, seq // tc),
        in_specs=[pl.BlockSpec((None, tc, w), lambda b, j: (b, j, 0)),
                  pl.BlockSpec((1, w), lambda b, j: (0, 0))],
        out_specs=[slab, slab],
        out_shape=[out, out],
        scratch_shapes=[pltpu.VMEM((1, w), F32)],
        compiler_params=_params("arbitrary", "arbitrary"),
        name="fcum",
    )(f3, bf_row)


ATTN_MASKED = 1
ATTN_LAST = 2


def _attn_tables(nq, tq, tk):
    qi_l, ki_l, fl_l = [], [], []
    for qi in range(nq):
        last = (qi * tq + tq - 1) // tk
        for ki in range(last + 1):
            masked = ki * tk + tk - 1 > qi * tq
            qi_l.append(qi)
            ki_l.append(ki)
            fl_l.append((ATTN_MASKED if masked else 0) | (ATTN_LAST if ki == last else 0))
    as_i32 = lambda v: jnp.asarray(np.asarray(v, np.int32))
    return as_i32(qi_l), as_i32(ki_l), as_i32(fl_l)


def _attn_kernel(qi_tab, ki_tab, fl_tab, q_ref, qb_ref, k_ref, kb_ref, v_ref, o_ref,
                 qop_sc, m_sc, l_sc, acc_sc, *, tq, tk, rb, dh, heads):
    p = pl.program_id(2)
    qi = qi_tab[p]
    ki = ki_tab[p]
    fl = fl_tab[p]
    head_of_lane = lax.broadcasted_iota(I32, (tq, LANES), 1) // dh

    @pl.when(ki == 0)
    def _():
        m_sc[...] = jnp.full(m_sc.shape, NEG_INF, F32)
        l_sc[...] = jnp.zeros(l_sc.shape, F32)
        acc_sc[...] = jnp.zeros(acc_sc.shape, F32)
        q = q_ref[...]
        for hh in range(heads):
            qop_sc[hh, :, :LANES] = jnp.where(head_of_lane == hh, q, jnp.zeros_like(q))
            qop_sc[hh, :, LANES:] = qb_ref[:, hh * LANES:(hh + 1) * LANES]

    def per_head(vals):
        out = vals[0]
        lane_head = lax.broadcasted_iota(I32, out.shape, 1) // dh
        for hh in range(1, heads):
            out = jnp.where(lane_head == hh, vals[hh], out)
        return out

    def step(diagonal):
        for r0 in range(0, tq, rb):
            rows = slice(r0, r0 + rb)
            nk = min(tk, r0 + rb) if diagonal else tk
            if diagonal:
                keep = (lax.broadcasted_iota(I32, (rb, nk), 1)
                        <= r0 + lax.broadcasted_iota(I32, (rb, nk), 0))
            alphas, pvs = [], []
            for hh in range(heads):
                kop = jnp.concatenate([k_ref[:nk, :], kb_ref[:nk, hh * LANES:(hh + 1) * LANES]],
                                      axis=1)
                s = lax.dot_general(qop_sc[hh, rows, :], kop, (((1,), (1,)), ((), ())),
                                    preferred_element_type=F32)
                if diagonal:
                    s = jnp.where(keep, s, NEG_INF)
                m_prev = m_sc[hh, rows, :]
                m_new = jnp.maximum(m_prev, jnp.max(s, axis=-1, keepdims=True))
                alpha = jnp.exp2(m_prev - m_new)
                pr = jnp.exp2(s - jnp.concatenate([m_new] * (nk // LANES), axis=1))
                l_sc[hh, rows, :] = alpha * l_sc[hh, rows, :] + jnp.sum(pr, axis=-1, keepdims=True)
                m_sc[hh, rows, :] = m_new
                alphas.append(alpha)
                pvs.append(jnp.dot(pr.astype(BF16), v_ref[:nk, :], preferred_element_type=F32))
            acc_sc[rows, :] = per_head(alphas) * acc_sc[rows, :] + per_head(pvs)

    pl.when((fl & ATTN_MASKED) != 0)(lambda: step(True))
    pl.when((fl & ATTN_MASKED) == 0)(lambda: step(False))

    @pl.when((fl & ATTN_LAST) != 0)
    def _():
        o_ref[...] = (acc_sc[...] / per_head([l_sc[hh] for hh in range(heads)])).astype(o_ref.dtype)


def _attention(qkv, qb, kb, *, bsz, seq, att_w, head_dim, tq, tk, rb):
    heads = LANES // head_dim
    ncol = att_w // LANES
    nq, nk = seq // tq, seq // tk
    assert tq == tk and tq % rb == 0 and rb % LANES == 0
    qi_tab, ki_tab, fl_tab = _attn_tables(nq, tq, tk)
    kern = functools.partial(_attn_kernel, tq=tq, tk=tk, rb=rb, dh=head_dim, heads=heads)
    qrow = lambda b, c, p, qt, kt, ft: b * nq + qt[p]
    krow = lambda b, c, p, qt, kt, ft: b * nk + kt[p]
    grid_spec = pltpu.PrefetchScalarGridSpec(
        num_scalar_prefetch=3,
        grid=(bsz, ncol, int(qi_tab.shape[0])),
        in_specs=[
            pl.BlockSpec((tq, LANES), lambda b, c, *a: (qrow(b, c, *a), c)),
            pl.BlockSpec((tq, heads * LANES), lambda b, c, *a: (qrow(b, c, *a), c)),
            pl.BlockSpec((tk, LANES), lambda b, c, *a: (krow(b, c, *a), ncol + c)),
            pl.BlockSpec((tk, heads * LANES), lambda b, c, *a: (krow(b, c, *a), c)),
            pl.BlockSpec((tk, LANES), lambda b, c, *a: (krow(b, c, *a), 2 * ncol + c)),
        ],
        out_specs=pl.BlockSpec((tq, LANES), lambda b, c, *a: (qrow(b, c, *a), c)),
        scratch_shapes=[pltpu.VMEM((heads, tq, 2 * LANES), BF16),
                        pltpu.VMEM((heads, tq, LANES), F32), pltpu.VMEM((heads, tq, LANES), F32),
                        pltpu.VMEM((tq, LANES), F32)],
    )
    return pl.pallas_call(
        kern,
        grid_spec=grid_spec,
        out_shape=jax.ShapeDtypeStruct((bsz * seq, att_w), BF16),
        compiler_params=_params("arbitrary", "arbitrary", "arbitrary"),
        name="attn",
    )(qi_tab, ki_tab, fl_tab, qkv, qb, qkv, kb, qkv)


CONV_HALO = 32
CONV_CHUNK = 64


SUBLANES = 8


def _conv_kernel(h_ref, halo_ref, w_ref, b_ref, g_ref, beta_ref, o_ref, sh_sc, *, ts, taps, tpb):
    first = pl.program_id(0) % tpb == 0
    rows = ts + CONV_HALO - SUBLANES
    sh_sc[0, 0:CONV_HALO, :] = jnp.where(first, 0.0, halo_ref[...])
    sh_sc[0, CONV_HALO:, :] = h_ref[...]
    for r in range(1, SUBLANES):
        sh_sc[r, 0:rows, :] = sh_sc[0, r:r + rows, :]
    lead = CONV_HALO - (taps - 1)
    for c0 in range(0, ts, CONV_CHUNK):
        acc = jnp.broadcast_to(b_ref[...], (CONV_CHUNK, b_ref.shape[1]))
        for k in range(taps):
            r = (lead + k) % SUBLANES
            start = c0 + lead + k - r
            acc = acc + w_ref[k:k + 1, :] * sh_sc[r, start:start + CONV_CHUNK, :]
        y = _normalize(acc) * g_ref[...] + beta_ref[...]
        o_ref[c0:c0 + CONV_CHUNK, :] = (y * jax.nn.sigmoid(y)).astype(o_ref.dtype)


def _conv(h, w, b, g, beta, *, seq, ts):
    t, c = h.shape
    taps = w.shape[0]
    assert taps - 1 <= CONV_HALO and ts % CONV_CHUNK == 0 and ts % CONV_HALO == 0
    tpb = seq // ts
    hb = ts // CONV_HALO
    vec = lambda v: v.reshape(1, c)
    return pl.pallas_call(
        functools.partial(_conv_kernel, ts=ts, taps=taps, tpb=tpb),
        grid=(t // ts,),
        in_specs=[pl.BlockSpec((ts, c), lambda i: (i, 0)),
                  pl.BlockSpec((CONV_HALO, c), lambda i: (jnp.maximum(i * hb - 1, 0), 0)),
                  _resident((taps, c)), _resident((1, c)), _resident((1, c)), _resident((1, c))],
        out_specs=pl.BlockSpec((ts, c), lambda i: (i, 0)),
        out_shape=jax.ShapeDtypeStruct((t, c), BF16),
        scratch_shapes=[pltpu.VMEM((SUBLANES, ts + CONV_HALO, c), F32)],
        compiler_params=_params("arbitrary"),
        name="conv",
    )(h, h, w, vec(b), vec(g), vec(beta))


def _merge_kernel(attn_ref, hc_ref, g_ref, x_ref, g1_ref, sh2_ref, sc2_ref, wa_ref, wc_ref,
                  wo_ref, lg_ref, lb_ref, wq_ref, x1_ref, u2_ref, qp_ref, *, alpha):
    d = x_ref.shape[1]
    ya = jnp.dot(attn_ref[...], wa_ref[...], preferred_element_type=F32)
    yb = jnp.dot(hc_ref[...], wc_ref[...], preferred_element_type=F32)
    merged = (g_ref[:, :d] * ya + g_ref[:, d:] * yb).astype(BF16)
    out = jnp.dot(merged, wo_ref[...], preferred_element_type=F32)
    x1 = _normalize(alpha * x_ref[...] + (1.0 + g1_ref[...]) * out) * lg_ref[...] + lb_ref[...]
    x1_ref[...] = x1
    u2 = (_normalize(x1) * (1.0 + sc2_ref[...]) + sh2_ref[...]).astype(BF16)
    u2_ref[...] = u2
    qp_ref[...] = jnp.dot(u2, wq_ref[...], preferred_element_type=F32).astype(qp_ref.dtype)


def _merge(attn, hc, gates, x2, g1, sh2, sc2, wa, wc, wo, lg, lb, wq, *, seq, alpha, tm):
    t, d = x2.shape
    tpb = seq // tm
    row = lambda i: (i, 0)
    per_batch = pl.BlockSpec((None, 1, d), lambda i: (i // tpb, 0, 0))
    nq = wq.shape[1]
    return pl.pallas_call(
        functools.partial(_merge_kernel, alpha=alpha),
        grid=(t // tm,),
        in_specs=[pl.BlockSpec((tm, attn.shape[1]), row), pl.BlockSpec((tm, hc.shape[1]), row),
                  pl.BlockSpec((tm, 2 * d), row), pl.BlockSpec((tm, d), row),
                  per_batch, per_batch, per_batch,
                  _resident(wa.shape), _resident(wc.shape), _resident(wo.shape),
                  _resident((1, d)), _resident((1, d)), _resident(wq.shape)],
        out_specs=[pl.BlockSpec((tm, d), row), pl.BlockSpec((tm, d), row),
                   pl.BlockSpec((tm, nq), row)],
        out_shape=[jax.ShapeDtypeStruct((t, d), F32), jax.ShapeDtypeStruct((t, d), BF16),
                   jax.ShapeDtypeStruct((t, nq), BF16)],
        compiler_params=_params("arbitrary"),
        name="merge",
    )(attn, hc, gates, x2, g1, sh2, sc2, wa, wc, wo, lg.reshape(1, d), lb.reshape(1, d), wq)


def _staircase(k):
    return [(a, b) for a in range(k) for b in range(k) if (a + 1) * (b + 1) <= k]


ID_PAD = 1e9


def _topk_columns(s, k, ids=None):
    n, tl = s.shape
    ng = n // SUBLANES
    vals = [s[SUBLANES * g:SUBLANES * (g + 1), :] for g in range(ng)]
    if ids is None:
        sub = lax.broadcasted_iota(I32, (SUBLANES, tl), 0).astype(F32)
        ids = [sub + float(SUBLANES * g) for g in range(ng)]
    else:
        ids = [ids[SUBLANES * g:SUBLANES * (g + 1), :] for g in range(ng)]
    for phase in range(ng):
        for a in range(phase % 2, ng - 1, 2):
            up = vals[a + 1] > vals[a]
            vals[a], vals[a + 1] = (jnp.maximum(vals[a], vals[a + 1]),
                                    jnp.minimum(vals[a], vals[a + 1]))
            ids[a], ids[a + 1] = (jnp.where(up, ids[a + 1], ids[a]),
                                  jnp.where(up, ids[a], ids[a + 1]))
    out_v, out_i = [], []
    for r in range(k):
        m = jnp.max(vals[0], axis=0, keepdims=True)
        i = jnp.min(jnp.where(vals[0] == m, ids[0], ID_PAD), axis=0, keepdims=True)
        out_v.append(m)
        out_i.append(i)
        win = ids[0] == i
        live = min(ng, k - r)
        for g in range(live - 1):
            vals[g] = jnp.where(win, vals[g + 1], vals[g])
            ids[g] = jnp.where(win, ids[g + 1], ids[g])
        if live == ng:
            vals[ng - 1] = jnp.where(win, NEG_INF, vals[ng - 1])
    return jnp.concatenate(out_v, axis=0), jnp.concatenate(out_i, axis=0)


RETRIEVE_HEADS = 2


def _retrieve_kernel(qp_ref, keys_ref, flat_ref, ii_ref, jj_ref, gg_ref, *, dhalf, nkeys, topk):
    tl = qp_ref.shape[0]
    qb = qp_ref[...]
    pairs = _staircase(topk)
    npad = flat_ref.shape[0]
    for hh in range(RETRIEVE_HEADS):
        sv, si = [], []
        for half in range(2):
            col = (2 * hh + half) * dhalf
            s = lax.dot_general(keys_ref[half], qb[:, col:col + dhalf],
                                (((1,), (1,)), ((), ())), preferred_element_type=F32)
            v, i = _topk_columns(s, topk)
            sv.append(v)
            si.append(i)

        cand = [sv[0][a:a + 1, :] + sv[1][b:b + 1, :] for a, b in pairs]
        cand += [jnp.full((1, tl), NEG_INF, F32)] * (npad - len(pairs))
        cand = jnp.concatenate(cand, axis=0)
        cv, ci = _topk_columns(cand, topk, flat_ref[...])

        ci = ci.astype(I32)
        ia = lax.shift_right_logical(ci, int(math.log2(topk)))
        ib = ci & (topk - 1)
        ii = jnp.zeros((topk, tl), F32)
        jj = jnp.zeros((topk, tl), F32)
        for a in range(topk):
            ii = jnp.where(ia == a, si[0][a:a + 1, :], ii)
            jj = jnp.where(ib == a, si[1][a:a + 1, :], jj)
        rows = slice(hh * topk, (hh + 1) * topk)
        ii_ref[rows, :] = ii.astype(I32)
        jj_ref[rows, :] = jj.astype(I32)
        e = jnp.exp(cv - cv[0:1, :])
        gg_ref[rows, :] = e / jnp.sum(e, axis=0, keepdims=True)


def _retrieve(qp, keys, *, tl):
    t, nq = qp.shape
    _, nkeys, dhalf = keys.shape
    heads = nq // (2 * dhalf)
    assert 2 * dhalf == LANES and PEER_TOPK & (PEER_TOPK - 1) == 0 and heads % RETRIEVE_HEADS == 0
    pairs = _staircase(PEER_TOPK)
    npad = -(-len(pairs) // 8) * 8
    flat = np.full((npad, tl), ID_PAD / 2, np.float32)
    flat[:len(pairs), :] = np.asarray([a * PEER_TOPK + b for a, b in pairs], np.float32)[:, None]
    kern = functools.partial(_retrieve_kernel, dhalf=dhalf, nkeys=nkeys, topk=PEER_TOPK)
    out = jax.ShapeDtypeStruct((heads * PEER_TOPK, t), I32)
    blk = pl.BlockSpec((RETRIEVE_HEADS * PEER_TOPK, tl), lambda i, h: (h, i))
    return pl.pallas_call(
        kern,
        grid=(t // tl, heads // RETRIEVE_HEADS),
        in_specs=[pl.BlockSpec((tl, RETRIEVE_HEADS * LANES), lambda i, h: (i, h)),
                  _resident(keys.shape),
                  _resident(flat.shape)],
        out_specs=[blk, blk, blk],
        out_shape=[out, out, jax.ShapeDtypeStruct((heads * PEER_TOPK, t), F32)],
        compiler_params=_params("arbitrary", "arbitrary"),
        name="retrieve",
    )(qp, keys, jnp.asarray(flat))


def _gelu_tanh(a):
    return 0.5 * a * (1.0 + jnp.tanh(math.sqrt(2.0 / math.pi) * (a + 0.044715 * a * a * a)))


def _peer_act_kernel(x_ref, u_ref, iit_ref, jjt_ref, ggt_ref, w_ref, ii_ref, jj_ref,
                     *, groups, nchunks):
    c = pl.program_id(1)

    @pl.when(c == 0)
    def _():
        ii_ref[...] = iit_ref[...].T
        jj_ref[...] = jjt_ref[...].T
        w_ref[...] = jnp.zeros(w_ref.shape, F32)

    a_all = lax.dot_general(x_ref[...], u_ref[...], (((1,), (1,)), ((), ())),
                            preferred_element_type=F32)
    ii = ii_ref[...]
    jj = jj_ref[...]
    acc = w_ref[...]
    for g in range(groups):
        picked = jnp.take_along_axis(a_all[:, g * LANES:(g + 1) * LANES], jj, axis=1)
        acc = jnp.where(ii == c * groups + g, picked, acc)
    w_ref[...] = acc

    @pl.when(c == nchunks - 1)
    def _():
        w_ref[...] = _gelu_tanh(w_ref[...]) * ggt_ref[...].T


def _peer_act(u2, utab, ii_t, jj_t, gg_t, *, tb, groups):
    t, d = u2.shape
    ne = utab.shape[0]
    hk = ii_t.shape[0]
    ce = groups * LANES
    nchunks = ne // ce
    tok = lambda i, c: (i, 0)
    hk_major = pl.BlockSpec((hk, tb), lambda i, c: (0, i))
    return pl.pallas_call(
        functools.partial(_peer_act_kernel, groups=groups, nchunks=nchunks),
        grid=(t // tb, nchunks),
        in_specs=[pl.BlockSpec((tb, d), tok), pl.BlockSpec((ce, d), lambda i, c: (c, 0)),
                  hk_major, hk_major, hk_major],
        out_specs=[pl.BlockSpec((tb, hk), tok)] * 3,
        out_shape=[jax.ShapeDtypeStruct((t, hk), F32), jax.ShapeDtypeStruct((t, hk), I32),
                   jax.ShapeDtypeStruct((t, hk), I32)],
        compiler_params=_params("arbitrary", "arbitrary"),
        name="peer_act",
    )(u2, utab, ii_t, jj_t, gg_t)


SCATTER_UNROLL = 32
SCATTER_TOKENS = 2


def _w_pitch(tb):
    p = -(-tb // 8)
    return 8 * (p if p % 2 else p + 1)


def _peer_out_kernel(w_ref, ii_ref, jj_ref, v_ref, x1_ref, g2_ref, lg_ref, lb_ref,
                     o_ref, w_sc, acc_sc, *, tb, groups, nchunks, alpha):
    c = pl.program_id(1)
    pitch = _w_pitch(tb)
    tpm = SCATTER_TOKENS

    @pl.when(c == 0)
    def _():
        acc_sc[...] = jnp.zeros(acc_sc.shape, F32)
        sub = lax.broadcasted_iota(I32, (LANES, LANES), 0)

        def scatter(grp, carry):
            toks = [grp * tpm + x for x in range(tpm)]
            lhs = jnp.concatenate(
                [jnp.where(sub == ii_ref[pl.ds(t_, 1), :], w_ref[pl.ds(t_, 1), :], 0.0).astype(BF16)
                 for t_ in toks], axis=0)
            rhs = jnp.concatenate(
                [jnp.where(sub == jj_ref[pl.ds(t_, 1), :], 1.0, 0.0).astype(BF16)
                 for t_ in toks], axis=0)
            out = lax.dot_general(lhs, rhs, (((1,), (1,)), ((), ())),
                                  preferred_element_type=F32)
            for x, t_ in enumerate(toks):
                blk = slice(x * LANES, (x + 1) * LANES)
                w_sc[pl.ds(t_, LANES, stride=pitch), :] = out[blk, blk]
            return carry

        lax.fori_loop(0, tb // tpm, scatter, 0, unroll=SCATTER_UNROLL // tpm)

    parts = [w_sc[pl.ds(pl.multiple_of((c * groups + g) * pitch, 8), tb), :].astype(BF16)
             for g in range(groups)]
    acc_sc[...] += jnp.dot(jnp.concatenate(parts, axis=1), v_ref[...],
                           preferred_element_type=F32)

    @pl.when(c == nchunks - 1)
    def _():
        r = alpha * x1_ref[...] + (1.0 + g2_ref[...]) * acc_sc[...]
        o_ref[...] = _normalize(r) * lg_ref[...] + lb_ref[...]


def _peer_out(w, ii, jj, vtab, x1, g2, lg, lb, *, seq, alpha, tb, groups):
    t, d = x1.shape
    ne = vtab.shape[0]
    assert ne == LANES * LANES
    ce = groups * LANES
    nchunks = ne // ce
    tpb = seq // tb
    tok = lambda i, c: (i, 0)
    assert tb % SCATTER_TOKENS == 0 and SCATTER_UNROLL % SCATTER_TOKENS == 0
    kern = functools.partial(_peer_out_kernel, tb=tb, groups=groups, nchunks=nchunks, alpha=alpha)
    return pl.pallas_call(
        kern,
        grid=(t // tb, nchunks),
        in_specs=[pl.BlockSpec((tb, LANES), tok), pl.BlockSpec((tb, LANES), tok),
                  pl.BlockSpec((tb, LANES), tok),
                  pl.BlockSpec((ce, d), lambda i, c: (c, 0)),
                  pl.BlockSpec((tb, d), tok, pipeline_mode=pl.Buffered(1)),
                  pl.BlockSpec((None, 1, d), lambda i, c: (i // tpb, 0, 0)),
                  _resident((1, d)), _resident((1, d))],
        out_specs=pl.BlockSpec((tb, d), tok),
        out_shape=jax.ShapeDtypeStruct((t, d), F32),
        scratch_shapes=[pltpu.VMEM((LANES * _w_pitch(tb), LANES), F32), pltpu.VMEM((tb, d), F32)],
        compiler_params=_params("arbitrary", "arbitrary"),
        name="peer_out",
    )(w, ii, jj, vtab, x1, g2, lg.reshape(1, d), lb.reshape(1, d))


def _tile(n, want):
    t = min(n, want)
    while n % t:
        t -= 1
    return t


def kernel(x, c, w_mod, b_mod, w_in, b_f, conv_w, conv_b, conv_ln_g, conv_ln_b,
           w_attn_out, w_conv_out, w_out, ln1_g, ln1_b, peer_wq, peer_sub_keys,
           peer_u, peer_v, ln2_g, ln2_b):
    bsz, seq, d = x.shape
    depth = w_mod.shape[0]
    t = bsz * seq
    att_w = w_attn_out.shape[1]
    n_heads = b_f.shape[1]
    head_dim = att_w // n_heads
    conv_c = conv_w.shape[2]
    alpha = (2.0 * depth) ** 0.25
    assert n_heads <= LANES and LANES % head_dim == 0 and att_w % LANES == 0

    tm = _tile(seq, 512)
    ta = _tile(seq, 2048)
    tp = _tile(seq, 1024)
    xf = x.reshape(t, d)
    for l in range(depth):
        mod = _mod(c, w_mod[l], b_mod[l])
        sh1, sc1, g1, sh2, sc2, g2 = [m.reshape(bsz, 1, d) for m in jnp.split(mod, 6, axis=-1)]

        o_f = 3 * att_w
        o_c = o_f + n_heads
        o_g = o_c + 2 * conv_c
        wl = w_in[l]
        wqkv = wl[:, :o_f].astype(BF16)
        wf = jnp.pad(wl[:, o_f:o_c], ((0, 0), (0, LANES - n_heads))).astype(BF16)
        wc = wl[:, o_c:o_g].astype(BF16)
        wg = wl[:, o_g:].astype(BF16)
        qkv, f, h, gates = _inproj(xf, sh1, sc1, wqkv, wf, wc, wg, seq=seq, att_w=att_w,
                                   conv_c=conv_c, head_dim=head_dim, tm=tm)

        bf_row = jnp.pad(b_f[l], (0, LANES - n_heads)).reshape(1, LANES)
        qb, kb = _fcum(f.reshape(bsz, seq, LANES), bf_row, n_heads=n_heads, tc=_tile(seq, 256))
        attn = _attention(qkv, qb.reshape(t, -1), kb.reshape(t, -1), bsz=bsz, seq=seq,
                          att_w=att_w, head_dim=head_dim, tq=ta, tk=ta, rb=_tile(ta, 512))

        hc = _conv(h, conv_w[l], conv_b[l], conv_ln_g[l], conv_ln_b[l], seq=seq, ts=tm)

        x1, u2, qp = _merge(attn, hc, gates, xf, g1, sh2, sc2,
                            w_attn_out[l].astype(BF16), w_conv_out[l].astype(BF16),
                            w_out[l].astype(BF16), ln1_g[l], ln1_b[l],
                            peer_wq[l].astype(BF16), seq=seq, alpha=alpha, tm=tm)

        ii_t, jj_t, gg_t = _retrieve(qp, peer_sub_keys[l].astype(BF16), tl=tm)
        w, ii, jj = _peer_act(u2, peer_u[l].astype(BF16), ii_t, jj_t, gg_t, tb=tp, groups=32)
        xf = _peer_out(w, ii, jj, peer_v[l].astype(BF16), x1, g2, ln2_g[l], ln2_b[l],
                       seq=seq, alpha=alpha, tb=tm, groups=16)
    return xf.reshape(bsz, seq, d)
```

```python
import functools
import math

import jax
import jax.numpy as jnp
import numpy as np
from jax import lax
from jax.experimental import pallas as pl
from jax.experimental.pallas import tpu as pltpu

F32 = jnp.float32
BF16 = jnp.bfloat16
I32 = jnp.int32

LN_EPS = 1e-5
PEER_TOPK = 16
LANES = 128
VMEM_LIMIT_BYTES = 56 * 1024 * 1024
NEG_INF = float("-inf")
LOG2_E = math.log2(math.e)


def _params(*semantics):
    return pltpu.CompilerParams(dimension_semantics=semantics, vmem_limit_bytes=VMEM_LIMIT_BYTES)


def _normalize(x):
    mu = jnp.mean(x, axis=-1, keepdims=True)
    xc = x - mu
    var = jnp.mean(xc * xc, axis=-1, keepdims=True)
    return xc * lax.rsqrt(var + LN_EPS)


def _resident(shape):
    nd = len(shape)
    return pl.BlockSpec(shape, lambda *_: (0,) * nd)


def _mod_kernel(c_ref, w_ref, b_ref, o_ref):
    c = c_ref[...]
    cs = c * jax.nn.sigmoid(c)
    o_ref[...] = jnp.dot(cs, w_ref[...], preferred_element_type=F32,
                         precision=lax.Precision.HIGHEST) + b_ref[...]


def _mod(c, w, b):
    bsz, d = c.shape
    n = w.shape[1]
    rows = -(-bsz // 8) * 8
    tn = math.gcd(n, 1536)
    cp = jnp.pad(c, ((0, rows - bsz), (0, 0)))
    out = pl.pallas_call(
        _mod_kernel,
        grid=(n // tn,),
        in_specs=[pl.BlockSpec((rows, d), lambda j: (0, 0)),
                  pl.BlockSpec((d, tn), lambda j: (0, j)),
                  pl.BlockSpec((1, tn), lambda j: (0, j))],
        out_specs=pl.BlockSpec((rows, tn), lambda j: (0, j)),
        out_shape=jax.ShapeDtypeStruct((rows, n), F32),
        compiler_params=_params("arbitrary"),
        name="mod",
    )(cp, w, b.reshape(1, n))
    return out[:bsz]


def _inproj_kernel(x_ref, sh_ref, sc_ref, wqkv_ref, wf_ref, wc_ref, wg_ref, bf_ref,
                   qkv_ref, qb_ref, kb_ref, h_ref, g_ref, carry_sc,
                   *, att_w, conv_c, qscale, tpb, n_heads):
    y = _normalize(x_ref[...])
    u = (y * (1.0 + sc_ref[...]) + sh_ref[...]).astype(BF16)
    qkv = jnp.dot(u, wqkv_ref[...], preferred_element_type=F32)
    qkv_ref[:, :att_w] = (qkv[:, :att_w] * qscale).astype(BF16)
    qkv_ref[:, att_w:] = qkv[:, att_w:].astype(BF16)
    f = jnp.dot(u, wf_ref[...], preferred_element_type=F32)
    _forget_bias_slabs(f, bf_ref, qb_ref, kb_ref, carry_sc,
                       first=pl.program_id(0) % tpb == 0, n_heads=n_heads)
    cab = jnp.dot(u, wc_ref[...], preferred_element_type=F32)
    h_ref[...] = cab[:, :conv_c] * jax.nn.sigmoid(cab[:, conv_c:])
    g_ref[...] = jax.nn.sigmoid(jnp.dot(u, wg_ref[...], preferred_element_type=F32))


def _inproj(x2, sh, sc, wqkv, wf, wc, wg, bf_row, *, seq, att_w, conv_c, head_dim, n_heads, tm):
    t, d = x2.shape
    tpb = seq // tm
    row = lambda i: (i, 0)
    per_batch = pl.BlockSpec((None, 1, d), lambda i: (i // tpb, 0, 0))
    kern = functools.partial(_inproj_kernel, att_w=att_w, conv_c=conv_c, tpb=tpb, n_heads=n_heads,
                             qscale=LOG2_E / math.sqrt(head_dim))
    slab = pl.BlockSpec((tm, n_heads * LANES), row)
    return pl.pallas_call(
        kern,
        grid=(t // tm,),
        in_specs=[pl.BlockSpec((tm, d), row), per_batch, per_batch,
                  _resident(wqkv.shape), _resident(wf.shape), _resident(wc.shape),
                  _resident(wg.shape), _resident(bf_row.shape)],
        out_specs=[pl.BlockSpec((tm, 3 * att_w), row), slab, slab,
                   pl.BlockSpec((tm, conv_c), row), pl.BlockSpec((tm, 2 * d), row)],
        out_shape=[jax.ShapeDtypeStruct((t, 3 * att_w), BF16),
                   jax.ShapeDtypeStruct((t, n_heads * LANES), BF16),
                   jax.ShapeDtypeStruct((t, n_heads * LANES), BF16),
                   jax.ShapeDtypeStruct((t, conv_c), F32),
                   jax.ShapeDtypeStruct((t, 2 * d), F32)],
        scratch_shapes=[pltpu.VMEM((1, LANES), F32)],
        compiler_params=_params("arbitrary"),
        name="inproj",
    )(x2, sh, sc, wqkv, wf, wc, wg, bf_row)


BIAS_TERMS = 3


def _forget_bias_slabs(f, bf_ref, qb_ref, kb_ref, carry_ref, *, first, n_heads):
    tc = f.shape[0]

    @pl.when(first)
    def _():
        carry_ref[...] = jnp.zeros(carry_ref.shape, F32)

    z = f + bf_ref[...]
    logf = jnp.minimum(z, 0.0) - jnp.log1p(jnp.exp(-jnp.abs(z)))
    r = lax.broadcasted_iota(I32, (tc, tc), 0)
    c = lax.broadcasted_iota(I32, (tc, tc), 1)
    tri = jnp.where(c <= r, 1.0, 0.0).astype(F32)
    cs = jnp.dot(tri, logf, preferred_element_type=F32,
                 precision=lax.Precision.HIGHEST) + carry_ref[...]
    carry_ref[...] = cs[tc - 1:tc, :]
    rem = cs * LOG2_E
    terms = []
    for _ in range(BIAS_TERMS):
        term = rem.astype(BF16)
        terms.append(term.astype(F32))
        rem = rem - terms[-1]
    lane = lax.broadcasted_iota(I32, (tc, LANES), 1)
    for hh in range(n_heads):
        qb = jnp.where((lane >= BIAS_TERMS) & (lane < 2 * BIAS_TERMS), -1.0, 0.0)
        kb = jnp.where(lane < BIAS_TERMS, 1.0, 0.0)
        for r, term in enumerate(terms):
            col = jnp.broadcast_to(term[:, hh:hh + 1], (tc, LANES))
            qb = jnp.where(lane == r, col, qb)
            kb = jnp.where(lane == BIAS_TERMS + r, col, kb)
        qb_ref[:, hh * LANES:(hh + 1) * LANES] = qb.astype(BF16)
        kb_ref[:, hh * LANES:(hh + 1) * LANES] = kb.astype(BF16)


ATTN_MASKED = 1
ATTN_LAST = 2


def _attn_tables(nq, tq, tk):
    qi_l, ki_l, fl_l = [], [], []
    for qi in range(nq):
        last = (qi * tq + tq - 1) // tk
        for ki in range(last + 1):
            masked = ki * tk + tk - 1 > qi * tq
            qi_l.append(qi)
            ki_l.append(ki)
            fl_l.append((ATTN_MASKED if masked else 0) | (ATTN_LAST if ki == last else 0))
    as_i32 = lambda v: jnp.asarray(np.asarray(v, np.int32))
    return as_i32(qi_l), as_i32(ki_l), as_i32(fl_l)


def _attn_kernel(qi_tab, ki_tab, fl_tab, q_ref, qb_ref, k_ref, kb_ref, v_ref, o_ref,
                 qop_sc, m_sc, l_sc, acc_sc, *, tq, tk, rb, dh, heads):
    p = pl.program_id(2)
    qi = qi_tab[p]
    ki = ki_tab[p]
    fl = fl_tab[p]
    head_of_lane = lax.broadcasted_iota(I32, (tq, LANES), 1) // dh

    @pl.when(ki == 0)
    def _():
        m_sc[...] = jnp.full(m_sc.shape, NEG_INF, F32)
        l_sc[...] = jnp.zeros(l_sc.shape, F32)
        acc_sc[...] = jnp.zeros(acc_sc.shape, F32)
        q = q_ref[...]
        for hh in range(heads):
            qop_sc[hh, :, :LANES] = jnp.where(head_of_lane == hh, q, jnp.zeros_like(q))
            qop_sc[hh, :, LANES:] = qb_ref[:, hh * LANES:(hh + 1) * LANES]

    def per_head(vals):
        out = vals[0]
        lane_head = lax.broadcasted_iota(I32, out.shape, 1) // dh
        for hh in range(1, heads):
            out = jnp.where(lane_head == hh, vals[hh], out)
        return out

    def step(diagonal):
        for r0 in range(0, tq, rb):
            rows = slice(r0, r0 + rb)
            nk = min(tk, r0 + rb) if diagonal else tk
            if diagonal:
                keep = (lax.broadcasted_iota(I32, (rb, nk), 1)
                        <= r0 + lax.broadcasted_iota(I32, (rb, nk), 0))
            alphas, pvs = [], []
            for hh in range(heads):
                kop = jnp.concatenate([k_ref[:nk, :], kb_ref[:nk, hh * LANES:(hh + 1) * LANES]],
                                      axis=1)
                s = lax.dot_general(qop_sc[hh, rows, :], kop, (((1,), (1,)), ((), ())),
                                    preferred_element_type=F32)
                if diagonal:
                    s = jnp.where(keep, s, NEG_INF)
                m_prev = m_sc[hh, rows, :]
                m_new = jnp.maximum(m_prev, jnp.max(s, axis=-1, keepdims=True))
                alpha = jnp.exp2(m_prev - m_new)
                pr = jnp.exp2(s - jnp.concatenate([m_new] * (nk // LANES), axis=1))
                l_sc[hh, rows, :] = alpha * l_sc[hh, rows, :] + jnp.sum(pr, axis=-1, keepdims=True)
                m_sc[hh, rows, :] = m_new
                alphas.append(alpha)
                pvs.append(jnp.dot(pr.astype(BF16), v_ref[:nk, :], preferred_element_type=F32))
            acc_sc[rows, :] = per_head(alphas) * acc_sc[rows, :] + per_head(pvs)

    pl.when((fl & ATTN_MASKED) != 0)(lambda: step(True))
    pl.when((fl & ATTN_MASKED) == 0)(lambda: step(False))

    @pl.when((fl & ATTN_LAST) != 0)
    def _():
        o_ref[...] = (acc_sc[...] / per_head([l_sc[hh] for hh in range(heads)])).astype(o_ref.dtype)


def _attention(qkv, qb, kb, *, bsz, seq, att_w, head_dim, tq, tk, rb):
    heads = LANES // head_dim
    ncol = att_w // LANES
    nq, nk = seq // tq, seq // tk
    assert tq == tk and tq % rb == 0 and rb % LANES == 0
    qi_tab, ki_tab, fl_tab = _attn_tables(nq, tq, tk)
    kern = functools.partial(_attn_kernel, tq=tq, tk=tk, rb=rb, dh=head_dim, heads=heads)
    qrow = lambda b, c, p, qt, kt, ft: b * nq + qt[p]
    krow = lambda b, c, p, qt, kt, ft: b * nk + kt[p]
    grid_spec = pltpu.PrefetchScalarGridSpec(
        num_scalar_prefetch=3,
        grid=(bsz, ncol, int(qi_tab.shape[0])),
        in_specs=[
            pl.BlockSpec((tq, LANES), lambda b, c, *a: (qrow(b, c, *a), c)),
            pl.BlockSpec((tq, heads * LANES), lambda b, c, *a: (qrow(b, c, *a), c)),
            pl.BlockSpec((tk, LANES), lambda b, c, *a: (krow(b, c, *a), ncol + c)),
            pl.BlockSpec((tk, heads * LANES), lambda b, c, *a: (krow(b, c, *a), c)),
            pl.BlockSpec((tk, LANES), lambda b, c, *a: (krow(b, c, *a), 2 * ncol + c)),
        ],
        out_specs=pl.BlockSpec((tq, LANES), lambda b, c, *a: (qrow(b, c, *a), c)),
        scratch_shapes=[pltpu.VMEM((heads, tq, 2 * LANES), BF16),
                        pltpu.VMEM((heads, tq, LANES), F32), pltpu.VMEM((heads, tq, LANES), F32),
                        pltpu.VMEM((tq, LANES), F32)],
    )
    return pl.pallas_call(
        kern,
        grid_spec=grid_spec,
        out_shape=jax.ShapeDtypeStruct((bsz * seq, att_w), BF16),
        compiler_params=_params("arbitrary", "arbitrary", "arbitrary"),
        name="attn",
    )(qi_tab, ki_tab, fl_tab, qkv, qb, qkv, kb, qkv)


CONV_HALO = 32
CONV_CHUNK = 128


SUBLANES = 8


def _conv_kernel(h_ref, halo_ref, w_ref, b_ref, g_ref, beta_ref, o_ref, sh_sc, *, ts, taps, tpb):
    first = pl.program_id(0) % tpb == 0
    rows = ts + CONV_HALO - SUBLANES
    sh_sc[0, 0:CONV_HALO, :] = jnp.where(first, 0.0, halo_ref[...])
    sh_sc[0, CONV_HALO:, :] = h_ref[...]
    for r in range(1, SUBLANES):
        sh_sc[r, 0:rows, :] = sh_sc[0, r:r + rows, :]
    lead = CONV_HALO - (taps - 1)
    for c0 in range(0, ts, CONV_CHUNK):
        acc = jnp.broadcast_to(b_ref[...], (CONV_CHUNK, b_ref.shape[1]))
        for k in range(taps):
            r = (lead + k) % SUBLANES
            start = c0 + lead + k - r
            acc = acc + w_ref[k:k + 1, :] * sh_sc[r, start:start + CONV_CHUNK, :]
        y = _normalize(acc) * g_ref[...] + beta_ref[...]
        o_ref[c0:c0 + CONV_CHUNK, :] = (y * jax.nn.sigmoid(y)).astype(o_ref.dtype)


def _conv(h, w, b, g, beta, *, seq, ts):
    t, c = h.shape
    taps = w.shape[0]
    assert taps - 1 <= CONV_HALO and ts % CONV_CHUNK == 0 and ts % CONV_HALO == 0
    tpb = seq // ts
    hb = ts // CONV_HALO
    vec = lambda v: v.reshape(1, c)
    return pl.pallas_call(
        functools.partial(_conv_kernel, ts=ts, taps=taps, tpb=tpb),
        grid=(t // ts,),
        in_specs=[pl.BlockSpec((ts, c), lambda i: (i, 0)),
                  pl.BlockSpec((CONV_HALO, c), lambda i: (jnp.maximum(i * hb - 1, 0), 0)),
                  _resident((taps, c)), _resident((1, c)), _resident((1, c)), _resident((1, c))],
        out_specs=pl.BlockSpec((ts, c), lambda i: (i, 0)),
        out_shape=jax.ShapeDtypeStruct((t, c), BF16),
        scratch_shapes=[pltpu.VMEM((SUBLANES, ts + CONV_HALO, c), F32)],
        compiler_params=_params("arbitrary"),
        name="conv",
    )(h, h, w, vec(b), vec(g), vec(beta))


def _merge_kernel(attn_ref, hc_ref, g_ref, x_ref, g1_ref, sh2_ref, sc2_ref, wa_ref, wc_ref,
                  wo_ref, lg_ref, lb_ref, wq_ref, x1_ref, u2_ref, qp_ref, *, alpha):
    d = x_ref.shape[1]
    ya = jnp.dot(attn_ref[...], wa_ref[...], preferred_element_type=F32)
    yb = jnp.dot(hc_ref[...], wc_ref[...], preferred_element_type=F32)
    merged = (g_ref[:, :d] * ya + g_ref[:, d:] * yb).astype(BF16)
    out = jnp.dot(merged, wo_ref[...], preferred_element_type=F32)
    x1 = _normalize(alpha * x_ref[...] + (1.0 + g1_ref[...]) * out) * lg_ref[...] + lb_ref[...]
    x1_ref[...] = x1
    u2 = (_normalize(x1) * (1.0 + sc2_ref[...]) + sh2_ref[...]).astype(BF16)
    u2_ref[...] = u2
    qp_ref[...] = jnp.dot(u2, wq_ref[...], preferred_element_type=F32).astype(qp_ref.dtype)


def _merge(attn, hc, gates, x2, g1, sh2, sc2, wa, wc, wo, lg, lb, wq, *, seq, alpha, tm):
    t, d = x2.shape
    tpb = seq // tm
    row = lambda i: (i, 0)
    per_batch = pl.BlockSpec((None, 1, d), lambda i: (i // tpb, 0, 0))
    nq = wq.shape[1]
    return pl.pallas_call(
        functools.partial(_merge_kernel, alpha=alpha),
        grid=(t // tm,),
        in_specs=[pl.BlockSpec((tm, attn.shape[1]), row), pl.BlockSpec((tm, hc.shape[1]), row),
                  pl.BlockSpec((tm, 2 * d), row), pl.BlockSpec((tm, d), row),
                  per_batch, per_batch, per_batch,
                  _resident(wa.shape), _resident(wc.shape), _resident(wo.shape),
                  _resident((1, d)), _resident((1, d)), _resident(wq.shape)],
        out_specs=[pl.BlockSpec((tm, d), row), pl.BlockSpec((tm, d), row),
                   pl.BlockSpec((tm, nq), row)],
        out_shape=[jax.ShapeDtypeStruct((t, d), F32), jax.ShapeDtypeStruct((t, d), BF16),
                   jax.ShapeDtypeStruct((t, nq), BF16)],
        compiler_params=_params("arbitrary"),
        name="merge",
    )(attn, hc, gates, x2, g1, sh2, sc2, wa, wc, wo, lg.reshape(1, d), lb.reshape(1, d), wq)


def _staircase(k):
    return [(a, b) for a in range(k) for b in range(k) if (a + 1) * (b + 1) <= k]


ID_PAD = 1e9


def _topk_columns(s, k, ids=None):
    n, tl = s.shape
    ng = n // SUBLANES
    vals = [s[SUBLANES * g:SUBLANES * (g + 1), :] for g in range(ng)]
    if ids is None:
        sub = lax.broadcasted_iota(I32, (SUBLANES, tl), 0).astype(F32)
        ids = [sub + float(SUBLANES * g) for g in range(ng)]
    else:
        ids = [ids[SUBLANES * g:SUBLANES * (g + 1), :] for g in range(ng)]
    for phase in range(ng):
        for a in range(phase % 2, ng - 1, 2):
            up = vals[a + 1] > vals[a]
            vals[a], vals[a + 1] = (jnp.maximum(vals[a], vals[a + 1]),
                                    jnp.minimum(vals[a], vals[a + 1]))
            ids[a], ids[a + 1] = (jnp.where(up, ids[a + 1], ids[a]),
                                  jnp.where(up, ids[a], ids[a + 1]))
    out_v, out_i = [], []
    for r in range(k):
        m = jnp.max(vals[0], axis=0, keepdims=True)
        i = jnp.min(jnp.where(vals[0] == m, ids[0], ID_PAD), axis=0, keepdims=True)
        out_v.append(m)
        out_i.append(i)
        win = ids[0] == i
        live = min(ng, k - r)
        for g in range(live - 1):
            vals[g] = jnp.where(win, vals[g + 1], vals[g])
            ids[g] = jnp.where(win, ids[g + 1], ids[g])
        if live == ng:
            vals[ng - 1] = jnp.where(win, NEG_INF, vals[ng - 1])
    return jnp.concatenate(out_v, axis=0), jnp.concatenate(out_i, axis=0)


RETRIEVE_HEADS = 4


def _retrieve_kernel(qp_ref, keys_ref, flat_ref, ii_ref, jj_ref, gg_ref, *, dhalf, nkeys, topk):
    tl = qp_ref.shape[0]
    qb = qp_ref[...]
    pairs = _staircase(topk)
    npad = flat_ref.shape[0]
    for hh in range(RETRIEVE_HEADS):
        sv, si = [], []
        for half in range(2):
            col = (2 * hh + half) * dhalf
            s = lax.dot_general(keys_ref[half], qb[:, col:col + dhalf],
                                (((1,), (1,)), ((), ())), preferred_element_type=F32)
            v, i = _topk_columns(s, topk)
            sv.append(v)
            si.append(i)

        cand = [sv[0][a:a + 1, :] + sv[1][b:b + 1, :] for a, b in pairs]
        cand += [jnp.full((1, tl), NEG_INF, F32)] * (npad - len(pairs))
        cand = jnp.concatenate(cand, axis=0)
        cv, ci = _topk_columns(cand, topk, flat_ref[...])

        ci = ci.astype(I32)
        ia = lax.shift_right_logical(ci, int(math.log2(topk)))
        ib = ci & (topk - 1)
        ii = jnp.zeros((topk, tl), F32)
        jj = jnp.zeros((topk, tl), F32)
        for a in range(topk):
            ii = jnp.where(ia == a, si[0][a:a + 1, :], ii)
            jj = jnp.where(ib == a, si[1][a:a + 1, :], jj)
        rows = slice(hh * topk, (hh + 1) * topk)
        ii_ref[rows, :] = ii.astype(I32)
        jj_ref[rows, :] = jj.astype(I32)
        e = jnp.exp(cv - cv[0:1, :])
        gg_ref[rows, :] = e / jnp.sum(e, axis=0, keepdims=True)


def _retrieve(qp, keys, *, tl):
    t, nq = qp.shape
    _, nkeys, dhalf = keys.shape
    heads = nq // (2 * dhalf)
    nheads = RETRIEVE_HEADS
    assert 2 * dhalf == LANES and PEER_TOPK & (PEER_TOPK - 1) == 0 and heads % nheads == 0
    pairs = _staircase(PEER_TOPK)
    npad = -(-len(pairs) // 8) * 8
    flat = np.full((npad, tl), ID_PAD / 2, np.float32)
    flat[:len(pairs), :] = np.asarray([a * PEER_TOPK + b for a, b in pairs], np.float32)[:, None]
    kern = functools.partial(_retrieve_kernel, dhalf=dhalf, nkeys=nkeys, topk=PEER_TOPK)
    out = jax.ShapeDtypeStruct((heads * PEER_TOPK, t), I32)
    blk = pl.BlockSpec((nheads * PEER_TOPK, tl), lambda i, h: (h, i))
    return pl.pallas_call(
        kern,
        grid=(t // tl, heads // nheads),
        in_specs=[pl.BlockSpec((tl, nheads * LANES), lambda i, h: (i, h)),
                  _resident(keys.shape),
                  _resident(flat.shape)],
        out_specs=[blk, blk, blk],
        out_shape=[out, out, jax.ShapeDtypeStruct((heads * PEER_TOPK, t), F32)],
        compiler_params=_params("arbitrary", "arbitrary"),
        name="retrieve",
    )(qp, keys, jnp.asarray(flat))


def _gelu_tanh(a):
    return 0.5 * a * (1.0 + jnp.tanh(math.sqrt(2.0 / math.pi) * (a + 0.044715 * a * a * a)))


def _peer_act_kernel(x_ref, u_ref, iit_ref, jjt_ref, ggt_ref, w_ref, ii_ref, jj_ref,
                     *, groups, nchunks):
    c = pl.program_id(1)

    @pl.when(c == 0)
    def _():
        ii_ref[...] = iit_ref[...].T
        jj_ref[...] = jjt_ref[...].T
        w_ref[...] = jnp.zeros(w_ref.shape, F32)

    a_all = lax.dot_general(x_ref[...], u_ref[...], (((1,), (1,)), ((), ())),
                            preferred_element_type=F32)
    ii = ii_ref[...]
    jj = jj_ref[...]
    acc = w_ref[...]
    for g in range(groups):
        picked = jnp.take_along_axis(a_all[:, g * LANES:(g + 1) * LANES], jj, axis=1)
        acc = jnp.where(ii == c * groups + g, picked, acc)
    w_ref[...] = acc

    @pl.when(c == nchunks - 1)
    def _():
        w_ref[...] = _gelu_tanh(w_ref[...]) * ggt_ref[...].T


def _peer_act(u2, utab, ii_t, jj_t, gg_t, *, tb, groups):
    t, d = u2.shape
    ne = utab.shape[0]
    hk = ii_t.shape[0]
    ce = groups * LANES
    nchunks = ne // ce
    tok = lambda i, c: (i, 0)
    hk_major = pl.BlockSpec((hk, tb), lambda i, c: (0, i))
    return pl.pallas_call(
        functools.partial(_peer_act_kernel, groups=groups, nchunks=nchunks),
        grid=(t // tb, nchunks),
        in_specs=[pl.BlockSpec((tb, d), tok), pl.BlockSpec((ce, d), lambda i, c: (c, 0)),
                  hk_major, hk_major, hk_major],
        out_specs=[pl.BlockSpec((tb, hk), tok)] * 3,
        out_shape=[jax.ShapeDtypeStruct((t, hk), F32), jax.ShapeDtypeStruct((t, hk), I32),
                   jax.ShapeDtypeStruct((t, hk), I32)],
        compiler_params=_params("arbitrary", "arbitrary"),
        name="peer_act",
    )(u2, utab, ii_t, jj_t, gg_t)


SCATTER_UNROLL = 128
SCATTER_TOKENS = 2


def _w_pitch(tb):
    p = -(-tb // 8)
    return 8 * (p if p % 2 else p + 1)


def _peer_out_kernel(w_ref, ii_ref, jj_ref, v_ref, x1_ref, g2_ref, lg_ref, lb_ref,
                     o_ref, w_sc, acc_sc, *, tb, groups, nchunks, alpha):
    c = pl.program_id(1)
    pitch = _w_pitch(tb)
    tpm = SCATTER_TOKENS

    @pl.when(c == 0)
    def _():
        acc_sc[...] = jnp.zeros(acc_sc.shape, F32)
        sub = lax.broadcasted_iota(I32, (LANES, LANES), 0)

        def scatter(grp, carry):
            toks = [grp * tpm + x for x in range(tpm)]
            lhs = jnp.concatenate(
                [jnp.where(sub == ii_ref[pl.ds(t_, 1), :], w_ref[pl.ds(t_, 1), :], 0.0).astype(BF16)
                 for t_ in toks], axis=0)
            rhs = jnp.concatenate(
                [jnp.where(sub == jj_ref[pl.ds(t_, 1), :], 1.0, 0.0).astype(BF16)
                 for t_ in toks], axis=0)
            out = lax.dot_general(lhs, rhs, (((1,), (1,)), ((), ())),
                                  preferred_element_type=F32)
            for x, t_ in enumerate(toks):
                blk = slice(x * LANES, (x + 1) * LANES)
                w_sc[pl.ds(t_, LANES, stride=pitch), :] = out[blk, blk]
            return carry

        lax.fori_loop(0, tb // tpm, scatter, 0, unroll=SCATTER_UNROLL // tpm)

    parts = [w_sc[pl.ds(pl.multiple_of((c * groups + g) * pitch, 8), tb), :].astype(BF16)
             for g in range(groups)]
    acc_sc[...] += jnp.dot(jnp.concatenate(parts, axis=1), v_ref[...],
                           preferred_element_type=F32)

    @pl.when(c == nchunks - 1)
    def _():
        r = alpha * x1_ref[...] + (1.0 + g2_ref[...]) * acc_sc[...]
        o_ref[...] = _normalize(r) * lg_ref[...] + lb_ref[...]


def _peer_out(w, ii, jj, vtab, x1, g2, lg, lb, *, seq, alpha, tb, groups):
    t, d = x1.shape
    ne = vtab.shape[0]
    assert ne == LANES * LANES
    ce = groups * LANES
    nchunks = ne // ce
    tpb = seq // tb
    tok = lambda i, c: (i, 0)
    assert tb % SCATTER_TOKENS == 0 and SCATTER_UNROLL % SCATTER_TOKENS == 0
    kern = functools.partial(_peer_out_kernel, tb=tb, groups=groups, nchunks=nchunks, alpha=alpha)
    return pl.pallas_call(
        kern,
        grid=(t // tb, nchunks),
        in_specs=[pl.BlockSpec((tb, LANES), tok), pl.BlockSpec((tb, LANES), tok),
                  pl.BlockSpec((tb, LANES), tok),
                  pl.BlockSpec((ce, d), lambda i, c: (c, 0)),
                  pl.BlockSpec((tb, d), tok, pipeline_mode=pl.Buffered(1)),
                  pl.BlockSpec((None, 1, d), lambda i, c: (i // tpb, 0, 0)),
                  _resident((1, d)), _resident((1, d))],
        out_specs=pl.BlockSpec((tb, d), tok),
        out_shape=jax.ShapeDtypeStruct((t, d), F32),
        scratch_shapes=[pltpu.VMEM((LANES * _w_pitch(tb), LANES), F32), pltpu.VMEM((tb, d), F32)],
        compiler_params=_params("arbitrary", "arbitrary"),
        name="peer_out",
    )(w, ii, jj, vtab, x1, g2, lg.reshape(1, d), lb.reshape(1, d))


def _tile(n, want):
    t = min(n, want)
    while n % t:
        t -= 1
    return t


def kernel(x, c, w_mod, b_mod, w_in, b_f, conv_w, conv_b, conv_ln_g, conv_ln_b,
           w_attn_out, w_conv_out, w_out, ln1_g, ln1_b, peer_wq, peer_sub_keys,
           peer_u, peer_v, ln2_g, ln2_b):
    bsz, seq, d = x.shape
    depth = w_mod.shape[0]
    t = bsz * seq
    att_w = w_attn_out.shape[1]
    n_heads = b_f.shape[1]
    head_dim = att_w // n_heads
    conv_c = conv_w.shape[2]
    alpha = (2.0 * depth) ** 0.25
    assert n_heads <= LANES and LANES % head_dim == 0 and att_w % LANES == 0

    tm = _tile(seq, 512)
    ta = _tile(seq, 2048)
    tp = _tile(seq, 1024)
    xf = x.reshape(t, d)
    for l in range(depth):
        mod = _mod(c, w_mod[l], b_mod[l])
        sh1, sc1, g1, sh2, sc2, g2 = [m.reshape(bsz, 1, d) for m in jnp.split(mod, 6, axis=-1)]

        o_f = 3 * att_w
        o_c = o_f + n_heads
        o_g = o_c + 2 * conv_c
        wl = w_in[l]
        wqkv = wl[:, :o_f].astype(BF16)
        wf = jnp.pad(wl[:, o_f:o_c], ((0, 0), (0, LANES - n_heads))).astype(BF16)
        wc = wl[:, o_c:o_g].astype(BF16)
        wg = wl[:, o_g:].astype(BF16)
        bf_row = jnp.pad(b_f[l], (0, LANES - n_heads)).reshape(1, LANES)
        qkv, qb, kb, h, gates = _inproj(xf, sh1, sc1, wqkv, wf, wc, wg, bf_row, seq=seq,
                                        att_w=att_w, conv_c=conv_c, head_dim=head_dim,
                                        n_heads=n_heads, tm=tm)
        attn = _attention(qkv, qb, kb, bsz=bsz, seq=seq,
                          att_w=att_w, head_dim=head_dim, tq=ta, tk=ta, rb=_tile(ta, 512))

        hc = _conv(h, conv_w[l], conv_b[l], conv_ln_g[l], conv_ln_b[l], seq=seq, ts=tm)

        x1, u2, qp = _merge(attn, hc, gates, xf, g1, sh2, sc2,
                            w_attn_out[l].astype(BF16), w_conv_out[l].astype(BF16),
                            w_out[l].astype(BF16), ln1_g[l], ln1_b[l],
                            peer_wq[l].astype(BF16), seq=seq, alpha=alpha, tm=tm)

        ii_t, jj_t, gg_t = _retrieve(qp, peer_sub_keys[l].astype(BF16), tl=tp)
        w, ii, jj = _peer_act(u2, peer_u[l].astype(BF16), ii_t, jj_t, gg_t, tb=tp, groups=32)
        xf = _peer_out(w, ii, jj, peer_v[l].astype(BF16), x1, g2, ln2_g[l], ln2_b[l],
                       seq=seq, alpha=alpha, tb=tm, groups=16)
    return xf.reshape(bsz, seq, d)
```
